```python
import math
import jax, jax.numpy as jnp
from jax import lax
import numpy as np

D_MODEL = 2048
BATCH = 4
SEQ = 2048
DEPTH = 4

MIX_WIDTH = D_MODEL
ATTN_WIDTH = MIX_WIDTH // 2
POOL_WIDTH = MIX_WIDTH - ATTN_WIDTH
HEAD_DIM = 64
N_HEADS = ATTN_WIDTH // (2 * HEAD_DIM)
POOL_WINDOWS = (2, 4, 8, 16)
N_POOL_GROUPS = len(POOL_WINDOWS)
POOL_GROUP_DIM = POOL_WIDTH // N_POOL_GROUPS
PROJ_WIDTH = 3 * ATTN_WIDTH + POOL_WIDTH
ROPE_THETA = 10000.0
Q_BLOCK = 128
NORM_EPS = 1e-6
D_FF = 256 * ((8 * D_MODEL // 3 + 255) // 256)
N_EXPERTS = 8
TOP_K = 2
D_FF_EXPERT = 7 * D_MODEL // 2
N_DENSE = (DEPTH + 1) // 2
N_MOE = DEPTH // 2
LAMBDA_STD = 0.1

kernel_name = "hybrid_diffattn_multiscale_pool_moe"


def rms_norm(x, g):
    xf = x.astype(jnp.float32)
    y = xf * lax.rsqrt(jnp.mean(xf * xf, axis=-1, keepdims=True) + NORM_EPS)
    return (y * g.astype(jnp.float32)).astype(x.dtype)


def rope_tables(positions):
    inv_freq = 1.0 / (ROPE_THETA ** (jnp.arange(0, HEAD_DIM, 2, dtype=jnp.float32) / HEAD_DIM))
    ang = positions.astype(jnp.float32)[..., None] * inv_freq
    return jnp.cos(ang)[:, :, None, None, :], jnp.sin(ang)[:, :, None, None, :]


def apply_rope(x, cos, sin):
    half = HEAD_DIM // 2
    x1, x2 = x[..., :half], x[..., half:]
    c, s = cos.astype(x.dtype), sin.astype(x.dtype)
    return jnp.concatenate([x1 * c - x2 * s, x2 * c + x1 * s], axis=-1)


def diff_attention(q, k, v, lam, lambda_init, subln_g):
    B, S = q.shape[0], q.shape[1]
    nb = S // Q_BLOCK
    scale = HEAD_DIM ** -0.5
    qh = q.transpose(0, 2, 3, 1, 4)
    kh = k.transpose(0, 2, 3, 1, 4)
    vh = v.transpose(0, 2, 1, 3)
    q_blocks = qh.reshape(B, N_HEADS, 2, nb, Q_BLOCK, HEAD_DIM).transpose(3, 0, 1, 2, 4, 5)
    starts = jnp.arange(nb, dtype=jnp.int32) * Q_BLOCK
    k_idx = jnp.arange(S, dtype=jnp.int32)

    def block(args):
        qb, start = args
        s = jnp.einsum("bhcqd,bhckd->bhcqk", qb, kh).astype(jnp.float32) * scale
        q_idx = start + jnp.arange(Q_BLOCK, dtype=jnp.int32)
        mask = k_idx[None, :] <= q_idx[:, None]
        p = jax.nn.softmax(jnp.where(mask, s, -1e30), axis=-1)
        a = p[:, :, 0] - lam * p[:, :, 1]
        return jnp.einsum("bhqk,bhkd->bhqd", a.astype(vh.dtype), vh)

    out = lax.map(block, (q_blocks, starts))
    out = out.transpose(1, 2, 0, 3, 4).reshape(B, N_HEADS, S, 2 * HEAD_DIM)
    out = rms_norm(out, subln_g) * (1.0 - lambda_init)
    return out.transpose(0, 2, 1, 3).reshape(B, S, N_HEADS * 2 * HEAD_DIM)


def multiscale_pool(u, w, b, scale):
    B, S = u.shape[0], u.shape[1]
    ug = u.astype(jnp.float32).reshape(B, S, N_POOL_GROUPS, POOL_GROUP_DIM)
    cs = jnp.cumsum(ug, axis=1)
    t = jnp.arange(S, dtype=jnp.int32)
    outs = []
    for g, win in enumerate(POOL_WINDOWS):
        c = cs[:, :, g]
        prev = jnp.pad(c, ((0, 0), (win, 0), (0, 0)))[:, :S]
        cnt = jnp.minimum(t + 1, win).astype(jnp.float32)[None, :, None]
        outs.append((c - prev) / cnt - ug[:, :, g])
    d = jnp.stack(outs, axis=2).astype(u.dtype)
    y = jnp.einsum("bsgc,gcd->bsgd", d, w) + b
    return y.reshape(B, S, POOL_WIDTH) * scale


def swiglu(h, w1, w3, w2):
    return (jax.nn.silu(h @ w1) * (h @ w3)) @ w2


def moe_swiglu(h, router_w, w1, w3, w2):
    logits = (h @ router_w).astype(jnp.float32)
    top_v, top_i = lax.top_k(logits, TOP_K)
    top_w = jax.nn.softmax(top_v, axis=-1)
    gate = jnp.sum(jax.nn.one_hot(top_i, N_EXPERTS, dtype=jnp.float32) * top_w[..., None], axis=-2)
    gate = gate.astype(h.dtype)
    y = jnp.zeros_like(h)
    for e in range(N_EXPERTS):
        y = y + gate[..., e:e + 1] * swiglu(h, w1[e], w3[e], w2[e])
    return y


def setup_inputs(seed: int = 0) -> dict:
    key = jax.random.key(seed)
    ks = jax.random.split(key, 24)

    def nrm(k, shape, fan_in):
        return jax.random.normal(k, shape, jnp.float32) * (fan_in ** -0.5)

    def gain(k, shape):
        return 1.0 + 0.05 * jax.random.normal(k, shape, jnp.float32)

    x = jax.random.normal(ks[0], (BATCH, SEQ, D_MODEL), jnp.float32)
    offs = jax.random.randint(ks[1], (BATCH, 1), 0, 512, dtype=jnp.int32)
    positions = offs + jnp.arange(SEQ, dtype=jnp.int32)[None, :]
    return {
        "x": x,
        "positions": positions,
        "attn_norm": gain(ks[2], (DEPTH, D_MODEL)),
        "w_in": nrm(ks[3], (DEPTH, D_MODEL, PROJ_WIDTH), D_MODEL),
        "q_norm": gain(ks[4], (DEPTH, HEAD_DIM)),
        "k_norm": gain(ks[5], (DEPTH, HEAD_DIM)),
        "lambda_q1": LAMBDA_STD * jax.random.normal(ks[6], (DEPTH, HEAD_DIM), jnp.float32),
        "lambda_k1": LAMBDA_STD * jax.random.normal(ks[7], (DEPTH, HEAD_DIM), jnp.float32),
        "lambda_q2": LAMBDA_STD * jax.random.normal(ks[8], (DEPTH, HEAD_DIM), jnp.float32),
        "lambda_k2": LAMBDA_STD * jax.random.normal(ks[9], (DEPTH, HEAD_DIM), jnp.float32),
        "subln": gain(ks[10], (DEPTH, 2 * HEAD_DIM)),
        "pool_w": nrm(ks[11], (DEPTH, N_POOL_GROUPS, POOL_GROUP_DIM, POOL_GROUP_DIM), POOL_GROUP_DIM),
        "pool_b": 0.02 * jax.random.normal(ks[12], (DEPTH, N_POOL_GROUPS, POOL_GROUP_DIM), jnp.float32),
        "pool_scale": gain(ks[13], (DEPTH, POOL_WIDTH)),
        "w_out": nrm(ks[14], (DEPTH, MIX_WIDTH, D_MODEL), MIX_WIDTH),
        "ffn_norm": gain(ks[15], (DEPTH, D_MODEL)),
        "dense_w1": nrm(ks[16], (N_DENSE, D_MODEL, D_FF), D_MODEL),
        "dense_w3": nrm(ks[17], (N_DENSE, D_MODEL, D_FF), D_MODEL),
        "dense_w2": nrm(ks[18], (N_DENSE, D_FF, D_MODEL), D_FF),
        "router_w": nrm(ks[19], (N_MOE, D_MODEL, N_EXPERTS), D_MODEL),
        "moe_w1": nrm(ks[20], (N_MOE, N_EXPERTS, D_MODEL, D_FF_EXPERT), D_MODEL),
        "moe_w3": nrm(ks[21], (N_MOE, N_EXPERTS, D_MODEL, D_FF_EXPERT), D_MODEL),
        "moe_w2": nrm(ks[22], (N_MOE, N_EXPERTS, D_FF_EXPERT, D_MODEL), D_FF_EXPERT),
    }


def reference(x, positions, attn_norm, w_in, q_norm, k_norm, lambda_q1, lambda_k1, lambda_q2,
              lambda_k2, subln, pool_w, pool_b, pool_scale, w_out, ffn_norm, dense_w1, dense_w3,
              dense_w2, router_w, moe_w1, moe_w3, moe_w2):
    B, S = x.shape[0], x.shape[1]
    cos, sin = rope_tables(positions)
    for l in range(DEPTH):
        h = rms_norm(x, attn_norm[l])
        proj = h @ w_in[l]
        q = proj[..., :ATTN_WIDTH].reshape(B, S, N_HEADS, 2, HEAD_DIM)
        k = proj[..., ATTN_WIDTH:2 * ATTN_WIDTH].reshape(B, S, N_HEADS, 2, HEAD_DIM)
        v = proj[..., 2 * ATTN_WIDTH:3 * ATTN_WIDTH].reshape(B, S, N_HEADS, 2 * HEAD_DIM)
        u = proj[..., 3 * ATTN_WIDTH:]
        q = apply_rope(rms_norm(q, q_norm[l]), cos, sin)
        k = apply_rope(rms_norm(k, k_norm[l]), cos, sin)
        lambda_init = 0.8 - 0.6 * math.exp(-0.3 * l)
        lam = (jnp.exp(jnp.sum(lambda_q1[l].astype(jnp.float32) * lambda_k1[l].astype(jnp.float32)))
               - jnp.exp(jnp.sum(lambda_q2[l].astype(jnp.float32) * lambda_k2[l].astype(jnp.float32)))
               + lambda_init)
        a = diff_attention(q, k, v, lam, lambda_init, subln[l])
        p = multiscale_pool(u, pool_w[l], pool_b[l], pool_scale[l])
        x = x + jnp.concatenate([a, p], axis=-1) @ w_out[l]
        h = rms_norm(x, ffn_norm[l])
        if l % 2 == 0:
            i = l // 2
            x = x + swiglu(h, dense_w1[i], dense_w3[i], dense_w2[i])
        else:
            i = l // 2
            x = x + moe_swiglu(h, router_w[i], moe_w1[i], moe_w3[i], moe_w2[i])
    return x
```

```python
import functools
import math

import jax
import jax.numpy as jnp
from jax import lax
from jax.experimental import pallas as pl
from jax.experimental.pallas import tpu as pltpu

F32 = jnp.float32
BF16 = jnp.bfloat16
I32 = jnp.int32

D_MODEL = 2048
ATTN_WIDTH = 1024
POOL_WIDTH = 1024
HEAD_DIM = 64
N_HEADS = 8
HEAD_WIDTH = 2 * HEAD_DIM
POOL_WINDOWS = (2, 4, 8, 16)
POOL_GROUP_DIM = 256
ROPE_THETA = 10000.0
NORM_EPS = 1e-6
N_EXPERTS = 8
TOP_K = 2
NEG_BIG = -1e30

LANES = 128
VMEM_BUDGET = 56 * 1024 * 1024

PROJ_BM = 1024
PROJ_BN = 1024
IN_PROJ_BN = 512
NORM_CHUNK = 128
ROPE_BM = 256
ATTN_BQ = 512
POOL_BM = 256
FFN_SUB = 256
FFN_BF = 256
DENSE_TILE = 1024
MOE_TILE = 2048
GATHER_BM = 256


def _cparams(sem, vmem=None):
    return pltpu.CompilerParams(dimension_semantics=sem, vmem_limit_bytes=vmem)


def _split_bf16(x):
    hi = x.astype(BF16)
    lo = (x - hi.astype(F32)).astype(BF16)
    return hi, lo


def _rms_rows(x_ref, g_ref, h_ref, rows):
    def body(c, carry):
        r = pl.multiple_of(c * NORM_CHUNK, NORM_CHUNK)
        x = x_ref[pl.ds(r, NORM_CHUNK), :]
        ms = jnp.mean(x * x, axis=-1, keepdims=True)
        h_ref[pl.ds(r, NORM_CHUNK), :] = (x * lax.rsqrt(ms + NORM_EPS) * g_ref[...]).astype(h_ref.dtype)
        return carry
    lax.fori_loop(0, rows // NORM_CHUNK, body, 0)


def _in_proj_kernel(x_ref, g_ref, w_ref, qk_ref, v_ref, u_ref, h_ref, *, n_qk, n_v):
    j = pl.program_id(1)

    @pl.when(j == 0)
    def _():
        _rms_rows(x_ref, g_ref, h_ref, PROJ_BM)

    y = jnp.dot(h_ref[...], w_ref[...].astype(BF16), preferred_element_type=F32)

    @pl.when(j < n_qk)
    def _():
        qk_ref[...] = y

    @pl.when((j >= n_qk) & (j < n_qk + n_v))
    def _():
        v_ref[...] = y.astype(BF16)

    @pl.when(j >= n_qk + n_v)
    def _():
        u_ref[...] = y


def _in_proj(x, g, w_all, layer):
    m, d = x.shape
    bn = IN_PROJ_BN
    n_qk = 2 * ATTN_WIDTH // bn
    n_v = ATTN_WIDTH // bn
    n_u = POOL_WIDTH // bn
    assert w_all.shape[2] == (n_qk + n_v + n_u) * bn and m % PROJ_BM == 0
    return pl.pallas_call(
        functools.partial(_in_proj_kernel, n_qk=n_qk, n_v=n_v),
        grid=(m // PROJ_BM, n_qk + n_v + n_u),
        in_specs=[
            pl.BlockSpec((PROJ_BM, d), lambda i, j: (i, 0)),
            pl.BlockSpec((1, d), lambda i, j: (0, 0)),
            pl.BlockSpec((None, d, bn), lambda i, j: (layer, 0, j)),
        ],
        out_specs=[
            pl.BlockSpec((PROJ_BM, bn), lambda i, j: (i, jnp.minimum(j, n_qk - 1))),
            pl.BlockSpec((PROJ_BM, bn), lambda i, j: (i, jnp.clip(j - n_qk, 0, n_v - 1))),
            pl.BlockSpec((PROJ_BM, bn), lambda i, j: (i, jnp.clip(j - n_qk - n_v, 0, n_u - 1))),
        ],
        out_shape=[
            jax.ShapeDtypeStruct((m, 2 * ATTN_WIDTH), F32),
            jax.ShapeDtypeStruct((m, ATTN_WIDTH), BF16),
            jax.ShapeDtypeStruct((m, POOL_WIDTH), F32),
        ],
        scratch_shapes=[pltpu.VMEM((PROJ_BM, d), BF16)],
        compiler_params=_cparams(("arbitrary", "arbitrary"), VMEM_BUDGET),
        name="in_proj",
    )(x, g, w_all)


def _qk_rope_kernel(pos_ref, invf_ref, qk_ref, gain_ref, o_ref):
    bm = qk_ref.shape[0]
    ang = pos_ref[...].astype(F32) * invf_ref[...]
    cos = jnp.cos(ang)
    sin = jnp.sin(ang)
    lane = lax.broadcasted_iota(I32, (bm, LANES), 1)
    first_half = (lane % HEAD_DIM) < (HEAD_DIM // 2)
    sin_signed = jnp.where(first_half, -sin, sin)
    gr = lax.broadcasted_iota(I32, (LANES, LANES), 0) // HEAD_DIM
    gc = lax.broadcasted_iota(I32, (LANES, LANES), 1) // HEAD_DIM
    group_ones = jnp.where(gr == gc, 1.0, 0.0).astype(BF16)
    n_blocks = qk_ref.shape[1] // LANES
    for hb in range(n_blocks):
        cols = slice(hb * LANES, (hb + 1) * LANES)
        x = qk_ref[:, cols]
        hi, lo = _split_bf16(x * x)
        ssum = (jnp.dot(hi, group_ones, preferred_element_type=F32)
                + jnp.dot(lo, group_ones, preferred_element_type=F32))
        y = x * lax.rsqrt(ssum * (1.0 / HEAD_DIM) + NORM_EPS) * gain_ref[:, cols]
        swapped = jnp.where(first_half, pltpu.roll(y, LANES - HEAD_DIM // 2, 1),
                            pltpu.roll(y, HEAD_DIM // 2, 1))
        r = y * cos + swapped * sin_signed
        if hb < n_blocks // 2:
            r = r * (HEAD_DIM ** -0.5)
        o_ref[:, cols] = r.astype(BF16)


def _qk_rope(pos, invf, qk, gain):
    m, w = qk.shape
    return pl.pallas_call(
        _qk_rope_kernel,
        grid=(m // ROPE_BM,),
        in_specs=[
            pl.BlockSpec((ROPE_BM, 1), lambda i: (i, 0)),
            pl.BlockSpec((1, LANES), lambda i: (0, 0)),
            pl.BlockSpec((ROPE_BM, w), lambda i: (i, 0)),
            pl.BlockSpec((1, w), lambda i: (0, 0)),
        ],
        out_specs=pl.BlockSpec((ROPE_BM, w), lambda i: (i, 0)),
        out_shape=jax.ShapeDtypeStruct((m, w), BF16),
        compiler_params=_cparams(("arbitrary",)),
        name="qk_rope",
    )(pos, invf, qk, gain)


def _attn_kernel(linit_ref, q_ref, k_ref, v_ref, lq1_ref, lk1_ref, lq2_ref, lk2_ref, sg_ref, o_ref,
                 m_ref, l_ref, acc_ref):
    qi = pl.program_id(2)
    bq = q_ref.shape[0]
    q = q_ref[...]
    lane = lax.broadcasted_iota(I32, (bq, HEAD_WIDTH), 1)
    zero = jnp.zeros_like(q)
    qc = (jnp.where(lane < HEAD_DIM, q, zero), jnp.where(lane >= HEAD_DIM, q, zero))

    m_ref[...] = jnp.full(m_ref.shape, NEG_BIG, F32)
    l_ref[...] = jnp.zeros(l_ref.shape, F32)
    acc_ref[...] = jnp.zeros(acc_ref.shape, F32)

    def block(kb, masked):
        r = pl.multiple_of(kb * bq, bq)
        k = k_ref[pl.ds(r, bq), :]
        v = v_ref[pl.ds(r, bq), :]
        if masked:
            row = lax.broadcasted_iota(I32, (bq, bq), 0)
            col = lax.broadcasted_iota(I32, (bq, bq), 1)
            keep = col <= row
        for c in range(2):
            s = lax.dot_general(qc[c], k, (((1,), (1,)), ((), ())), preferred_element_type=F32)
            if masked:
                s = jnp.where(keep, s, NEG_BIG)
            m_old = m_ref[c]
            m_new = jnp.maximum(m_old, jnp.max(s, axis=-1, keepdims=True))
            alpha = jnp.exp(m_old - m_new)
            p = jnp.exp(s - m_new)
            l_ref[c] = alpha * l_ref[c] + jnp.sum(p, axis=-1, keepdims=True)
            acc_ref[c] = alpha * acc_ref[c] + jnp.dot(p.astype(BF16), v, preferred_element_type=F32)
            m_ref[c] = m_new

    def off_diag(kb, carry):
        block(kb, False)
        return carry

    lax.fori_loop(0, qi, off_diag, 0)
    block(qi, True)

    lambda_init = linit_ref[0]
    lam = (jnp.exp(jnp.sum(lq1_ref[...] * lk1_ref[...], axis=-1, keepdims=True))
           - jnp.exp(jnp.sum(lq2_ref[...] * lk2_ref[...], axis=-1, keepdims=True))
           + lambda_init)
    o = acc_ref[0] / l_ref[0] - lam * (acc_ref[1] / l_ref[1])
    ms = jnp.mean(o * o, axis=-1, keepdims=True)
    o = o * lax.rsqrt(ms + NORM_EPS) * sg_ref[...]
    o_ref[...] = (o * (1.0 - lambda_init)).astype(BF16)


def _attention(linit, qk, v, lq1, lk1, lq2, lk2, sg, batch, seq):
    m = qk.shape[0]
    nq = seq // ATTN_BQ
    vec = pl.BlockSpec((1, HEAD_DIM), lambda b, h, i: (0, 0))
    return pl.pallas_call(
        _attn_kernel,
        grid=(batch, N_HEADS, nq),
        in_specs=[
            pl.BlockSpec(memory_space=pltpu.SMEM),
            pl.BlockSpec((ATTN_BQ, HEAD_WIDTH), lambda b, h, i: (b * nq + i, h)),
            pl.BlockSpec((seq, HEAD_WIDTH), lambda b, h, i: (b, N_HEADS + h)),
            pl.BlockSpec((seq, HEAD_WIDTH), lambda b, h, i: (b, h)),
            vec, vec, vec, vec,
            pl.BlockSpec((1, HEAD_WIDTH), lambda b, h, i: (0, 0)),
        ],
        out_specs=pl.BlockSpec((ATTN_BQ, HEAD_WIDTH), lambda b, h, i: (b * nq + i, h)),
        out_shape=jax.ShapeDtypeStruct((m, ATTN_WIDTH), BF16),
        scratch_shapes=[
            pltpu.VMEM((2, ATTN_BQ, 1), F32),
            pltpu.VMEM((2, ATTN_BQ, 1), F32),
            pltpu.VMEM((2, ATTN_BQ, HEAD_WIDTH), F32),
        ],
        compiler_params=_cparams(("arbitrary", "arbitrary", "arbitrary")),
        name="diff_attn",
    )(linit, qk, qk, v, lq1, lk1, lq2, lk2, sg)


def _pool_kernel(uc_ref, up_ref, w_ref, b_ref, sc_ref, o_ref, *, chunks_per_seq):
    c = pl.program_id(0) % chunks_per_seq
    bm = uc_ref.shape[0]
    row = lax.broadcasted_iota(I32, (bm, bm), 0)
    col = lax.broadcasted_iota(I32, (bm, bm), 1)
    t = c * bm + lax.broadcasted_iota(I32, (bm, 1), 0)
    has_prev = c > 0
    for g, win in enumerate(POOL_WINDOWS):
        cols = slice(g * POOL_GROUP_DIM, (g + 1) * POOL_GROUP_DIM)
        cur = jnp.where((row >= col) & (row - col < win), 1.0, 0.0).astype(BF16)
        prv = jnp.where(col - row > bm - win, 1.0, 0.0).astype(BF16)
        u = uc_ref[:, cols]
        u_hi, u_lo = _split_bf16(u)
        p_hi, p_lo = _split_bf16(up_ref[:, cols])
        wsum = (jnp.dot(cur, u_hi, preferred_element_type=F32)
                + jnp.dot(cur, u_lo, preferred_element_type=F32))
        wprev = (jnp.dot(prv, p_hi, preferred_element_type=F32)
                 + jnp.dot(prv, p_lo, preferred_element_type=F32))
        wsum = wsum + jnp.where(has_prev, wprev, 0.0)
        cnt = jnp.minimum(t + 1, win).astype(F32)
        d = wsum / cnt - u
        y = jnp.dot(d.astype(BF16), w_ref[g].astype(BF16), preferred_element_type=F32) + b_ref[:, cols]
        o_ref[:, cols] = (y * sc_ref[:, cols]).astype(BF16)


def _pool(u, w_all, layer, b, sc, seq):
    m, width = u.shape
    cps = seq // POOL_BM
    return pl.pallas_call(
        functools.partial(_pool_kernel, chunks_per_seq=cps),
        grid=(m // POOL_BM,),
        in_specs=[
            pl.BlockSpec((POOL_BM, width), lambda i: (i, 0)),
            pl.BlockSpec((POOL_BM, width), lambda i: (jnp.maximum(i - 1, 0), 0)),
            pl.BlockSpec((None,) + w_all.shape[1:], lambda i: (layer, 0, 0, 0)),
            pl.BlockSpec((1, width), lambda i: (0, 0)),
            pl.BlockSpec((1, width), lambda i: (0, 0)),
        ],
        out_specs=pl.BlockSpec((POOL_BM, width), lambda i: (i, 0)),
        out_shape=jax.ShapeDtypeStruct((m, width), BF16),
        compiler_params=_cparams(("arbitrary",)),
        name="pool_mixer",
    )(u, u, w_all, b, sc)


def _out_proj_kernel(a_ref, p_ref, wa_ref, wp_ref, x_ref, o_ref):
    y = jnp.dot(a_ref[...], wa_ref[...].astype(BF16), preferred_element_type=F32)
    y = y + jnp.dot(p_ref[...], wp_ref[...].astype(BF16), preferred_element_type=F32)
    o_ref[...] = x_ref[...] + y


def _out_proj(a, p, w_all, layer, x):
    m, d = x.shape
    ka = a.shape[1]
    kp = p.shape[1]
    assert ka == kp
    return pl.pallas_call(
        _out_proj_kernel,
        grid=(m // PROJ_BM, d // PROJ_BN),
        in_specs=[
            pl.BlockSpec((PROJ_BM, ka), lambda i, j: (i, 0)),
            pl.BlockSpec((PROJ_BM, kp), lambda i, j: (i, 0)),
            pl.BlockSpec((None, ka, PROJ_BN), lambda i, j: (layer, 0, j)),
            pl.BlockSpec((None, kp, PROJ_BN), lambda i, j: (layer, 1, j)),
            pl.BlockSpec((PROJ_BM, PROJ_BN), lambda i, j: (i, j)),
        ],
        out_specs=pl.BlockSpec((PROJ_BM, PROJ_BN), lambda i, j: (i, j)),
        out_shape=jax.ShapeDtypeStruct((m, d), F32),
        compiler_params=_cparams(("arbitrary", "arbitrary"), VMEM_BUDGET),
        name="out_proj",
    )(a, p, w_all, w_all, x)


def _ffn_norm_kernel(x_ref, g_ref, h_ref):
    _rms_rows(x_ref, g_ref, h_ref, x_ref.shape[0])


def _ffn_norm(x, g):
    m, d = x.shape
    bm = 512
    return pl.pallas_call(
        _ffn_norm_kernel,
        grid=(m // bm,),
        in_specs=[pl.BlockSpec((bm, d), lambda i: (i, 0)), pl.BlockSpec((1, d), lambda i: (0, 0))],
        out_specs=pl.BlockSpec((bm, d), lambda i: (i, 0)),
        out_shape=jax.ShapeDtypeStruct((m, d), BF16),
        compiler_params=_cparams(("arbitrary",)),
        name="ffn_norm",
    )(x, g)


def _route_kernel(x_ref, g_ref, rw_ref, h_ref, idx_ref, gate_ref):
    x = x_ref[...]
    ms = jnp.mean(x * x, axis=-1, keepdims=True)
    h = x * lax.rsqrt(ms + NORM_EPS) * g_ref[...]
    h_ref[...] = h
    h_hi, h_lo = _split_bf16(h)
    w_hi, w_lo = _split_bf16(rw_ref[...])
    logits = (jnp.dot(h_hi, w_hi, preferred_element_type=F32)
              + jnp.dot(h_hi, w_lo, preferred_element_type=F32)
              + jnp.dot(h_lo, w_hi, preferred_element_type=F32))
    lane = lax.broadcasted_iota(I32, logits.shape, 1)
    logits = jnp.where(lane < N_EXPERTS, logits, NEG_BIG)
    v1 = jnp.max(logits, axis=-1, keepdims=True)
    i1 = jnp.min(jnp.where(logits == v1, lane, LANES), axis=-1, keepdims=True)
    rest = jnp.where(lane == i1, NEG_BIG, logits)
    v2 = jnp.max(rest, axis=-1, keepdims=True)
    i2 = jnp.min(jnp.where(rest == v2, lane, LANES), axis=-1, keepdims=True)
    e = jnp.exp(v2 - v1)
    g1 = 1.0 / (1.0 + e)
    g2 = e / (1.0 + e)
    idx_ref[...] = jnp.where(lane == 0, i1, jnp.where(lane == 1, i2, 0))
    gate_ref[...] = jnp.where(lane == 0, g1, jnp.where(lane == 1, g2, 0.0))


def _route(x, g, rw_padded):
    m, d = x.shape
    bm = 256
    return pl.pallas_call(
        _route_kernel,
        grid=(m // bm,),
        in_specs=[
            pl.BlockSpec((bm, d), lambda i: (i, 0)),
            pl.BlockSpec((1, d), lambda i: (0, 0)),
            pl.BlockSpec((d, LANES), lambda i: (0, 0)),
        ],
        out_specs=[
            pl.BlockSpec((bm, d), lambda i: (i, 0)),
            pl.BlockSpec((bm, LANES), lambda i: (i, 0)),
            pl.BlockSpec((bm, LANES), lambda i: (i, 0)),
        ],
        out_shape=[
            jax.ShapeDtypeStruct((m, d), F32),
            jax.ShapeDtypeStruct((m, LANES), I32),
            jax.ShapeDtypeStruct((m, LANES), F32),
        ],
        compiler_params=_cparams(("arbitrary",)),
        name="ffn_norm_route",
    )(x, g, rw_padded)


def _swiglu_kernel(eid_ref, nsub_ref, blk_ref, tot_ref, x_ref, w1_ref, w3_ref, w2_ref, *rest, has_res):
    if has_res:
        res_ref, o_ref, w1s, w3s, w2s = rest
    else:
        o_ref, w1s, w3s, w2s = rest
    t = pl.program_id(0)
    j = pl.program_id(1)
    nsub = nsub_ref[t]

    @pl.when(nsub > 0)
    def _():
        w1s[...] = w1_ref[0].astype(BF16)
        w3s[...] = w3_ref[0].astype(BF16)
        w2s[...] = w2_ref[0].astype(BF16)

        def body(i, carry):
            r = pl.multiple_of(i * FFN_SUB, FFN_SUB)
            rows = pl.ds(r, FFN_SUB)
            xs = x_ref[rows, :]
            a = jnp.dot(xs, w1s[...], preferred_element_type=F32)
            b = jnp.dot(xs, w3s[...], preferred_element_type=F32)
            hidden = (a * jax.nn.sigmoid(a) * b).astype(BF16)
            y = jnp.dot(hidden, w2s[...], preferred_element_type=F32)

            @pl.when(j == 0)
            def _():
                if has_res:
                    o_ref[rows, :] = res_ref[rows, :] + y
                else:
                    o_ref[rows, :] = y

            @pl.when(j > 0)
            def _():
                o_ref[rows, :] += y

            return carry

        lax.fori_loop(0, nsub, body, 0)

    if not has_res:
        @pl.when(j == 0)
        def _():
            def clear(i, carry):
                r = pl.multiple_of(i * FFN_SUB, FFN_SUB)
                o_ref[pl.ds(r, FFN_SUB), :] = jnp.zeros((FFN_SUB, o_ref.shape[1]), F32)
                return carry
            lax.fori_loop(nsub, o_ref.shape[0] // FFN_SUB, clear, 0)


def _swiglu(meta, x, w1, w3, w2, res, tile_rows):
    eid, nsub, blk, tot = meta
    n_tiles = eid.shape[0]
    d = x.shape[1]
    f = w1.shape[2]
    n_j = f // FFN_BF
    has_res = res is not None

    def row_map(t, j, eid, nsub, blk, tot):
        return (blk[t], 0)

    def out_map(t, j, eid, nsub, blk, tot):
        return (t, 0)

    def hidden_block(t, j, tot):
        return jnp.where(t < tot[0], j, n_j - 1)

    def w13_map(t, j, eid, nsub, blk, tot):
        return (eid[t], 0, hidden_block(t, j, tot))

    def w2_map(t, j, eid, nsub, blk, tot):
        return (eid[t], hidden_block(t, j, tot), 0)

    once = pl.Buffered(1)
    in_specs = [
        pl.BlockSpec((tile_rows, d), row_map, pipeline_mode=once),
        pl.BlockSpec((1, d, FFN_BF), w13_map),
        pl.BlockSpec((1, d, FFN_BF), w13_map),
        pl.BlockSpec((1, FFN_BF, d), w2_map),
    ]
    args = [x, w1, w3, w2]
    if has_res:
        in_specs.append(pl.BlockSpec((tile_rows, d), row_map, pipeline_mode=once))
        args.append(res)
    return pl.pallas_call(
        functools.partial(_swiglu_kernel, has_res=has_res),
        grid_spec=pltpu.PrefetchScalarGridSpec(
            num_scalar_prefetch=4,
            grid=(n_tiles, n_j),
            in_specs=in_specs,
            out_specs=pl.BlockSpec((tile_rows, d), out_map, pipeline_mode=once),
            scratch_shapes=[
                pltpu.VMEM((d, FFN_BF), BF16),
                pltpu.VMEM((d, FFN_BF), BF16),
                pltpu.VMEM((FFN_BF, d), BF16),
            ],
        ),
        out_shape=jax.ShapeDtypeStruct((n_tiles * tile_rows, d), F32),
        compiler_params=_cparams(("arbitrary", "arbitrary"), VMEM_BUDGET),
        name="swiglu_res" if has_res else "swiglu_moe",
    )(eid, nsub, blk, tot, *args)


def _gather_kernel(nvalid_ref, tok_ref, h_hbm, o_ref, buf, sem):
    i = pl.program_id(0)
    n = o_ref.shape[0]
    valid = nvalid_ref[i] > 0

    def row_copy(r):
        return pltpu.make_async_copy(h_hbm.at[pl.ds(tok_ref[0, 0, r], 1)], buf.at[pl.ds(r, 1)], sem)

    @pl.when(valid)
    def _():
        def start(r, c):
            row_copy(r).start()
            return c
        lax.fori_loop(0, n, start, 0)

        def wait(r, c):
            row_copy(r).wait()
            return c
        lax.fori_loop(0, n, wait, 0)
        o_ref[...] = buf[...].astype(BF16)

    @pl.when(jnp.logical_not(valid))
    def _():
        o_ref[...] = jnp.zeros(o_ref.shape, BF16)


def _gather_rows(nvalid_blocks, tok_of_row, h):
    n_rows = tok_of_row.shape[0]
    d = h.shape[1]
    nb = n_rows // GATHER_BM
    tok3 = tok_of_row.reshape(nb, 1, GATHER_BM)
    return pl.pallas_call(
        _gather_kernel,
        grid_spec=pltpu.PrefetchScalarGridSpec(
            num_scalar_prefetch=1,
            grid=(nb,),
            in_specs=[
                pl.BlockSpec((1, 1, GATHER_BM), lambda i, nv: (i, 0, 0), memory_space=pltpu.SMEM),
                pl.BlockSpec(memory_space=pl.ANY),
            ],
            out_specs=pl.BlockSpec((GATHER_BM, d), lambda i, nv: (i, 0)),
            scratch_shapes=[pltpu.VMEM((GATHER_BM, d), F32), pltpu.SemaphoreType.DMA(())],
        ),
        out_shape=jax.ShapeDtypeStruct((n_rows, d), BF16),
        compiler_params=_cparams(("arbitrary",)),
        name="moe_gather",
    )(nvalid_blocks, tok3, h)


def _combine_kernel(pos_ref, x_ref, gate_ref, y_hbm, o_ref, buf, sem):
    n = o_ref.shape[0]

    def row_copy(r, s):
        return pltpu.make_async_copy(y_hbm.at[pl.ds(pos_ref[0, s, r], 1)], buf.at[s, pl.ds(r, 1)], sem)

    def start(r, c):
        row_copy(r, 0).start()
        row_copy(r, 1).start()
        return c
    lax.fori_loop(0, n, start, 0)

    def wait(r, c):
        row_copy(r, 0).wait()
        row_copy(r, 1).wait()
        return c
    lax.fori_loop(0, n, wait, 0)
    g = gate_ref[...]
    o_ref[...] = x_ref[...] + g[:, 0:1] * buf[0] + g[:, 1:2] * buf[1]


def _combine(pos, x, gate, y):
    m, d = x.shape
    nb = m // GATHER_BM
    pos3 = pos.reshape(nb, GATHER_BM, TOP_K).transpose(0, 2, 1)
    return pl.pallas_call(
        _combine_kernel,
        grid=(nb,),
        in_specs=[
            pl.BlockSpec((1, TOP_K, GATHER_BM), lambda i: (i, 0, 0), memory_space=pltpu.SMEM),
            pl.BlockSpec((GATHER_BM, d), lambda i: (i, 0)),
            pl.BlockSpec((GATHER_BM, LANES), lambda i: (i, 0)),
            pl.BlockSpec(memory_space=pl.ANY),
        ],
        out_specs=pl.BlockSpec((GATHER_BM, d), lambda i: (i, 0)),
        out_shape=jax.ShapeDtypeStruct((m, d), F32),
        scratch_shapes=[pltpu.VMEM((TOP_K, GATHER_BM, d), F32), pltpu.SemaphoreType.DMA(())],
        compiler_params=_cparams(("arbitrary",)),
        name="moe_combine",
    )(pos3, x, gate, y)


def _dispatch_plan(idx, n_tokens):
    n_assign = n_tokens * TOP_K
    n_tiles = n_assign // MOE_TILE + N_EXPERTS
    e_flat = idx.reshape(n_assign)
    onehot = (e_flat[:, None] == jnp.arange(N_EXPERTS, dtype=I32)[None, :]).astype(I32)
    counts = jnp.sum(onehot, axis=0)
    rank = jnp.take_along_axis(jnp.cumsum(onehot, axis=0) - onehot, e_flat[:, None], axis=1)[:, 0]
    tiles_e = (counts + MOE_TILE - 1) // MOE_TILE
    tile_end = jnp.cumsum(tiles_e)
    tile_start = tile_end - tiles_e
    total = tile_end[-1]
    pos = tile_start[e_flat] * MOE_TILE + rank
    t = jnp.arange(n_tiles, dtype=I32)
    t_eff = jnp.minimum(t, total - 1)
    eid = jnp.minimum(jnp.searchsorted(tile_end, t_eff, side="right").astype(I32), N_EXPERTS - 1)
    rows_valid = jnp.clip(counts[eid] - (t_eff - tile_start[eid]) * MOE_TILE, 0, MOE_TILE)
    rows_valid = jnp.where(t < total, rows_valid, 0)
    nsub = (rows_valid + FFN_SUB - 1) // FFN_SUB
    sub_per_tile = MOE_TILE // GATHER_BM
    sub_id = jnp.arange(n_tiles * sub_per_tile, dtype=I32)
    nvalid_blocks = ((sub_id % sub_per_tile) < nsub[sub_id // sub_per_tile]).astype(I32)
    tok_of_row = jnp.zeros((n_tiles * MOE_TILE,), I32).at[pos].set(jnp.arange(n_assign, dtype=I32) // TOP_K)
    meta = (eid, nsub.astype(I32), t_eff, jnp.reshape(total, (1,)).astype(I32))
    return meta, nvalid_blocks, tok_of_row, pos.reshape(n_tokens, TOP_K)


def _moe(x, g, rw, w1_all, w3_all, w2_all, layer_idx):
    m, d = x.shape
    w1 = w1_all.reshape((-1,) + w1_all.shape[2:])
    w3 = w3_all.reshape((-1,) + w3_all.shape[2:])
    w2 = w2_all.reshape((-1,) + w2_all.shape[2:])
    rw_padded = jnp.zeros((d, LANES), F32).at[:, :N_EXPERTS].set(rw)
    h, idx, gate = _route(x, g, rw_padded)
    meta, nvalid_blocks, tok_of_row, pos = _dispatch_plan(idx[:, :TOP_K], m)
    meta = (meta[0] + layer_idx * N_EXPERTS,) + meta[1:]
    xs = _gather_rows(nvalid_blocks, tok_of_row, h)
    y = _swiglu(meta, xs, w1, w3, w2, None, MOE_TILE)
    return _combine(pos, x, gate, y)


def _dense(x, g, w1_all, w3_all, w2_all, layer_idx):
    m, d = x.shape
    h = _ffn_norm(x, g)
    n_tiles = m // DENSE_TILE
    meta = (jnp.full((n_tiles,), layer_idx, I32), jnp.full((n_tiles,), DENSE_TILE // FFN_SUB, I32),
            jnp.arange(n_tiles, dtype=I32), jnp.full((1,), n_tiles, I32))
    return _swiglu(meta, h, w1_all, w3_all, w2_all, x, DENSE_TILE)


def kernel(x, positions, attn_norm, w_in, q_norm, k_norm, lambda_q1, lambda_k1, lambda_q2, lambda_k2, subln, pool_w, pool_b, pool_scale, w_out, ffn_norm, dense_w1, dense_w3, dense_w2, router_w, moe_w1, moe_w3, moe_w2):
    batch, seq, d = x.shape
    depth = w_in.shape[0]
    m = batch * seq
    xf = x.reshape(m, d)
    pos = positions.reshape(m, 1)
    inv_freq = 1.0 / (ROPE_THETA ** (jnp.arange(0, HEAD_DIM, 2, dtype=F32) / HEAD_DIM))
    invf = jnp.tile(inv_freq, LANES // (HEAD_DIM // 2))[None, :]
    for l in range(depth):
        qk, v, u = _in_proj(xf, attn_norm[l][None, :], w_in, l)
        gain = jnp.concatenate([jnp.tile(q_norm[l], ATTN_WIDTH // HEAD_DIM),
                                jnp.tile(k_norm[l], ATTN_WIDTH // HEAD_DIM)])[None, :]
        qk = _qk_rope(pos, invf, qk, gain)
        lambda_init = 0.8 - 0.6 * math.exp(-0.3 * l)
        a = _attention(jnp.full((1,), lambda_init, F32), qk, v, lambda_q1[l][None, :], lambda_k1[l][None, :],
                       lambda_q2[l][None, :], lambda_k2[l][None, :], subln[l][None, :], batch, seq)
        p = _pool(u, pool_w, l, pool_b[l].reshape(1, POOL_WIDTH), pool_scale[l][None, :], seq)
        xf = _out_proj(a, p, w_out, l, xf)
        i = l // 2
        if l % 2 == 0:
            xf = _dense(xf, ffn_norm[l][None, :], dense_w1, dense_w3, dense_w2, i)
        else:
            xf = _moe(xf, ffn_norm[l][None, :], router_w[i], moe_w1, moe_w3, moe_w2, i)
    return xf.reshape(batch, seq, d)
```

```python
import functools
import math

import jax
import jax.numpy as jnp
from jax import lax
from jax.experimental import pallas as pl
from jax.experimental.pallas import tpu as pltpu

F32 = jnp.float32
BF16 = jnp.bfloat16
I32 = jnp.int32

D_MODEL = 2048
ATTN_WIDTH = 1024
POOL_WIDTH = 1024
HEAD_DIM = 64
N_HEADS = 8
HEAD_WIDTH = 2 * HEAD_DIM
POOL_WINDOWS = (2, 4, 8, 16)
POOL_GROUP_DIM = 256
ROPE_THETA = 10000.0
NORM_EPS = 1e-6
N_EXPERTS = 8
TOP_K = 2
NEG_BIG = -1e30
LOG2_E = math.log2(math.e)

LANES = 128
VMEM_BUDGET = 56 * 1024 * 1024

PROJ_BM = 1024
PROJ_BN = 1024
IN_PROJ_BN = 512
NORM_CHUNK = 128
ROPE_BM = 256
ATTN_BQ = 512
POOL_BM = 256
FFN_SUB = 256
FFN_BF = 256
DENSE_TILE = 1024
MOE_TILE = 2048
GATHER_BM = 256


def _cparams(sem, vmem=None):
    return pltpu.CompilerParams(dimension_semantics=sem, vmem_limit_bytes=vmem)


def _split_bf16(x):
    hi = x.astype(BF16)
    lo = (x - hi.astype(F32)).astype(BF16)
    return hi, lo


def _rms_rows(x_ref, g_ref, h_ref, rows):
    def body(c, carry):
        r = pl.multiple_of(c * NORM_CHUNK, NORM_CHUNK)
        x = x_ref[pl.ds(r, NORM_CHUNK), :]
        ms = jnp.mean(x * x, axis=-1, keepdims=True)
        h_ref[pl.ds(r, NORM_CHUNK), :] = (x * lax.rsqrt(ms + NORM_EPS) * g_ref[...]).astype(h_ref.dtype)
        return carry
    lax.fori_loop(0, rows // NORM_CHUNK, body, 0)


def _in_proj_kernel(x_ref, g_ref, w_ref, qk_ref, v_ref, u_ref, h_ref, *, n_qk, n_v):
    j = pl.program_id(1)

    @pl.when(j == 0)
    def _():
        _rms_rows(x_ref, g_ref, h_ref, PROJ_BM)

    y = jnp.dot(h_ref[...], w_ref[...].astype(BF16), preferred_element_type=F32)

    @pl.when(j < n_qk)
    def _():
        qk_ref[...] = y

    @pl.when((j >= n_qk) & (j < n_qk + n_v))
    def _():
        v_ref[...] = y.astype(BF16)

    @pl.when(j >= n_qk + n_v)
    def _():
        u_ref[...] = y


def _in_proj(x, g, w_all, layer):
    m, d = x.shape
    bn = IN_PROJ_BN
    n_qk = 2 * ATTN_WIDTH // bn
    n_v = ATTN_WIDTH // bn
    n_u = POOL_WIDTH // bn
    assert w_all.shape[2] == (n_qk + n_v + n_u) * bn and m % PROJ_BM == 0
    return pl.pallas_call(
        functools.partial(_in_proj_kernel, n_qk=n_qk, n_v=n_v),
        grid=(m // PROJ_BM, n_qk + n_v + n_u),
        in_specs=[
            pl.BlockSpec((PROJ_BM, d), lambda i, j: (i, 0)),
            pl.BlockSpec((1, d), lambda i, j: (0, 0)),
            pl.BlockSpec((None, d, bn), lambda i, j: (layer, 0, j)),
        ],
        out_specs=[
            pl.BlockSpec((PROJ_BM, bn), lambda i, j: (i, jnp.minimum(j, n_qk - 1))),
            pl.BlockSpec((PROJ_BM, bn), lambda i, j: (i, jnp.clip(j - n_qk, 0, n_v - 1))),
            pl.BlockSpec((PROJ_BM, bn), lambda i, j: (i, jnp.clip(j - n_qk - n_v, 0, n_u - 1))),
        ],
        out_shape=[
            jax.ShapeDtypeStruct((m, 2 * ATTN_WIDTH), F32),
            jax.ShapeDtypeStruct((m, ATTN_WIDTH), BF16),
            jax.ShapeDtypeStruct((m, POOL_WIDTH), F32),
        ],
        scratch_shapes=[pltpu.VMEM((PROJ_BM, d), BF16)],
        compiler_params=_cparams(("arbitrary", "arbitrary"), VMEM_BUDGET),
        name="in_proj",
    )(x, g, w_all)


def _qk_rope_kernel(pos_ref, invf_ref, qk_ref, gain_ref, o_ref):
    bm = qk_ref.shape[0]
    ang = pos_ref[...].astype(F32) * invf_ref[...]
    cos = jnp.cos(ang)
    sin = jnp.sin(ang)
    lane = lax.broadcasted_iota(I32, (bm, LANES), 1)
    first_half = (lane % HEAD_DIM) < (HEAD_DIM // 2)
    sin_signed = jnp.where(first_half, -sin, sin)
    gr = lax.broadcasted_iota(I32, (LANES, LANES), 0) // HEAD_DIM
    gc = lax.broadcasted_iota(I32, (LANES, LANES), 1) // HEAD_DIM
    group_ones = jnp.where(gr == gc, 1.0, 0.0).astype(BF16)
    n_blocks = qk_ref.shape[1] // LANES
    for hb in range(n_blocks):
        cols = slice(hb * LANES, (hb + 1) * LANES)
        x = qk_ref[:, cols]
        hi, lo = _split_bf16(x * x)
        ssum = (jnp.dot(hi, group_ones, preferred_element_type=F32)
                + jnp.dot(lo, group_ones, preferred_element_type=F32))
        y = x * lax.rsqrt(ssum * (1.0 / HEAD_DIM) + NORM_EPS) * gain_ref[:, cols]
        swapped = jnp.where(first_half, pltpu.roll(y, LANES - HEAD_DIM // 2, 1),
                            pltpu.roll(y, HEAD_DIM // 2, 1))
        r = y * cos + swapped * sin_signed
        if hb < n_blocks // 2:
            r = r * (HEAD_DIM ** -0.5 * LOG2_E)
        o_ref[:, cols] = r.astype(BF16)


def _qk_rope(pos, invf, qk, gain):
    m, w = qk.shape
    return pl.pallas_call(
        _qk_rope_kernel,
        grid=(m // ROPE_BM,),
        in_specs=[
            pl.BlockSpec((ROPE_BM, 1), lambda i: (i, 0)),
            pl.BlockSpec((1, LANES), lambda i: (0, 0)),
            pl.BlockSpec((ROPE_BM, w), lambda i: (i, 0)),
            pl.BlockSpec((1, w), lambda i: (0, 0)),
        ],
        out_specs=pl.BlockSpec((ROPE_BM, w), lambda i: (i, 0)),
        out_shape=jax.ShapeDtypeStruct((m, w), BF16),
        compiler_params=_cparams(("arbitrary",)),
        name="qk_rope",
    )(pos, invf, qk, gain)


def _attn_kernel(linit_ref, q_ref, k_ref, v_ref, lq1_ref, lk1_ref, lq2_ref, lk2_ref, sg_ref, o_ref,
                 q2_ref, vt_ref, m_ref, l_ref, acc_ref):
    qi = pl.program_id(2)
    bq = q_ref.shape[0]

    @pl.when(qi == 0)
    def _():
        for c in range(vt_ref.shape[0]):
            vt_ref[c] = v_ref[c * bq:(c + 1) * bq, :].astype(F32).T.astype(BF16)

    q = q_ref[...]
    lane = lax.broadcasted_iota(I32, (bq, HEAD_WIDTH), 1)
    zero = jnp.zeros_like(q)
    q2_ref[0:bq, :] = jnp.where(lane < HEAD_DIM, q, zero)
    q2_ref[bq:2 * bq, :] = jnp.where(lane >= HEAD_DIM, q, zero)

    m_ref[...] = jnp.full(m_ref.shape, NEG_BIG, F32)
    l_ref[...] = jnp.zeros(l_ref.shape, F32)
    acc_ref[...] = jnp.zeros(acc_ref.shape, F32)

    def block(kb, masked):
        r = pl.multiple_of(kb * bq, bq)
        k = k_ref[pl.ds(r, bq), :]
        s = lax.dot_general(k, q2_ref[...], (((1,), (1,)), ((), ())), preferred_element_type=F32)
        if masked:
            key = lax.broadcasted_iota(I32, s.shape, 0)
            qry = lax.broadcasted_iota(I32, s.shape, 1) & (bq - 1)
            s = jnp.where(key <= qry, s, NEG_BIG)
        m_old = m_ref[...]
        m_new = jnp.maximum(m_old, jnp.max(s, axis=0, keepdims=True))
        alpha = jnp.exp2(m_old - m_new)
        p = jnp.exp2(s - m_new)
        l_ref[...] = alpha * l_ref[...] + jnp.sum(p, axis=0, keepdims=True)
        acc_ref[...] = alpha * acc_ref[...] + jnp.dot(vt_ref[kb], p.astype(BF16), preferred_element_type=F32)
        m_ref[...] = m_new

    def off_diag(kb, carry):
        block(kb, False)
        return carry

    lax.fori_loop(0, qi, off_diag, 0)
    block(qi, True)

    lambda_init = linit_ref[0]
    lam = (jnp.exp(jnp.sum(lq1_ref[...] * lk1_ref[...], axis=-1, keepdims=True))
           - jnp.exp(jnp.sum(lq2_ref[...] * lk2_ref[...], axis=-1, keepdims=True))
           + lambda_init)
    acc = acc_ref[...]
    l = l_ref[...]
    o = acc[:, :bq] / l[:, :bq] - lam * (acc[:, bq:] / l[:, bq:])
    ms = jnp.mean(o * o, axis=0, keepdims=True)
    o = o * lax.rsqrt(ms + NORM_EPS) * sg_ref[...] * (1.0 - lambda_init)
    o_ref[...] = o.T.astype(BF16)


def _attention(linit, qk, v, lq1, lk1, lq2, lk2, sg_col, batch, seq):
    m = qk.shape[0]
    nq = seq // ATTN_BQ
    vec = pl.BlockSpec((1, HEAD_DIM), lambda b, h, i: (0, 0))
    return pl.pallas_call(
        _attn_kernel,
        grid=(batch, N_HEADS, nq),
        in_specs=[
            pl.BlockSpec(memory_space=pltpu.SMEM),
            pl.BlockSpec((ATTN_BQ, HEAD_WIDTH), lambda b, h, i: (b * nq + i, h)),
            pl.BlockSpec((seq, HEAD_WIDTH), lambda b, h, i: (b, N_HEADS + h)),
            pl.BlockSpec((seq, HEAD_WIDTH), lambda b, h, i: (b, h)),
            vec, vec, vec, vec,
            pl.BlockSpec((HEAD_WIDTH, 1), lambda b, h, i: (0, 0)),
        ],
        out_specs=pl.BlockSpec((ATTN_BQ, HEAD_WIDTH), lambda b, h, i: (b * nq + i, h)),
        out_shape=jax.ShapeDtypeStruct((m, ATTN_WIDTH), BF16),
        scratch_shapes=[
            pltpu.VMEM((2 * ATTN_BQ, HEAD_WIDTH), BF16),
            pltpu.VMEM((nq, HEAD_WIDTH, ATTN_BQ), BF16),
            pltpu.VMEM((1, 2 * ATTN_BQ), F32),
            pltpu.VMEM((1, 2 * ATTN_BQ), F32),
            pltpu.VMEM((HEAD_WIDTH, 2 * ATTN_BQ), F32),
        ],
        compiler_params=_cparams(("arbitrary", "arbitrary", "arbitrary")),
        name="diff_attn",
    )(linit, qk, qk, v, lq1, lk1, lq2, lk2, sg_col)


def _pool_kernel(uc_ref, up_ref, w_ref, b_ref, sc_ref, o_ref, *, chunks_per_seq):
    c = pl.program_id(0) % chunks_per_seq
    bm = uc_ref.shape[0]
    row = lax.broadcasted_iota(I32, (bm, bm), 0)
    col = lax.broadcasted_iota(I32, (bm, bm), 1)
    t = c * bm + lax.broadcasted_iota(I32, (bm, 1), 0)
    has_prev = c > 0
    for g, win in enumerate(POOL_WINDOWS):
        cols = slice(g * POOL_GROUP_DIM, (g + 1) * POOL_GROUP_DIM)
        cur = jnp.where((row >= col) & (row - col < win), 1.0, 0.0).astype(BF16)
        prv = jnp.where(col - row > bm - win, 1.0, 0.0).astype(BF16)
        u = uc_ref[:, cols]
        u_hi, u_lo = _split_bf16(u)
        p_hi, p_lo = _split_bf16(up_ref[:, cols])
        wsum = (jnp.dot(cur, u_hi, preferred_element_type=F32)
                + jnp.dot(cur, u_lo, preferred_element_type=F32))
        wprev = (jnp.dot(prv, p_hi, preferred_element_type=F32)
                 + jnp.dot(prv, p_lo, preferred_element_type=F32))
        wsum = wsum + jnp.where(has_prev, wprev, 0.0)
        cnt = jnp.minimum(t + 1, win).astype(F32)
        d = wsum / cnt - u
        y = jnp.dot(d.astype(BF16), w_ref[g].astype(BF16), preferred_element_type=F32) + b_ref[:, cols]
        o_ref[:, cols] = (y * sc_ref[:, cols]).astype(BF16)


def _pool(u, w_all, layer, b, sc, seq):
    m, width = u.shape
    cps = seq // POOL_BM
    return pl.pallas_call(
        functools.partial(_pool_kernel, chunks_per_seq=cps),
        grid=(m // POOL_BM,),
        in_specs=[
            pl.BlockSpec((POOL_BM, width), lambda i: (i, 0)),
            pl.BlockSpec((POOL_BM, width), lambda i: (jnp.maximum(i - 1, 0), 0)),
            pl.BlockSpec((None,) + w_all.shape[1:], lambda i: (layer, 0, 0, 0)),
            pl.BlockSpec((1, width), lambda i: (0, 0)),
            pl.BlockSpec((1, width), lambda i: (0, 0)),
        ],
        out_specs=pl.BlockSpec((POOL_BM, width), lambda i: (i, 0)),
        out_shape=jax.ShapeDtypeStruct((m, width), BF16),
        compiler_params=_cparams(("arbitrary",)),
        name="pool_mixer",
    )(u, u, w_all, b, sc)


def _out_proj_kernel(a_ref, p_ref, wa_ref, wp_ref, x_ref, o_ref):
    y = jnp.dot(a_ref[...], wa_ref[...].astype(BF16), preferred_element_type=F32)
    y = y + jnp.dot(p_ref[...], wp_ref[...].astype(BF16), preferred_element_type=F32)
    o_ref[...] = x_ref[...] + y


def _out_proj(a, p, w_all, layer, x):
    m, d = x.shape
    ka = a.shape[1]
    kp = p.shape[1]
    assert ka == kp
    return pl.pallas_call(
        _out_proj_kernel,
        grid=(m // PROJ_BM, d // PROJ_BN),
        in_specs=[
            pl.BlockSpec((PROJ_BM, ka), lambda i, j: (i, 0)),
            pl.BlockSpec((PROJ_BM, kp), lambda i, j: (i, 0)),
            pl.BlockSpec((None, ka, PROJ_BN), lambda i, j: (layer, 0, j)),
            pl.BlockSpec((None, kp, PROJ_BN), lambda i, j: (layer, 1, j)),
            pl.BlockSpec((PROJ_BM, PROJ_BN), lambda i, j: (i, j)),
        ],
        out_specs=pl.BlockSpec((PROJ_BM, PROJ_BN), lambda i, j: (i, j)),
        out_shape=jax.ShapeDtypeStruct((m, d), F32),
        compiler_params=_cparams(("arbitrary", "arbitrary"), VMEM_BUDGET),
        name="out_proj",
    )(a, p, w_all, w_all, x)


def _ffn_norm_kernel(x_ref, g_ref, h_ref):
    _rms_rows(x_ref, g_ref, h_ref, x_ref.shape[0])


def _ffn_norm(x, g):
    m, d = x.shape
    bm = 512
    return pl.pallas_call(
        _ffn_norm_kernel,
        grid=(m // bm,),
        in_specs=[pl.BlockSpec((bm, d), lambda i: (i, 0)), pl.BlockSpec((1, d), lambda i: (0, 0))],
        out_specs=pl.BlockSpec((bm, d), lambda i: (i, 0)),
        out_shape=jax.ShapeDtypeStruct((m, d), BF16),
        compiler_params=_cparams(("arbitrary",)),
        name="ffn_norm",
    )(x, g)


def _route_kernel(x_ref, g_ref, rw_ref, h_ref, idx_ref, gate_ref):
    x = x_ref[...]
    ms = jnp.mean(x * x, axis=-1, keepdims=True)
    h = x * lax.rsqrt(ms + NORM_EPS) * g_ref[...]
    h_ref[...] = h
    h_hi, h_lo = _split_bf16(h)
    w_hi, w_lo = _split_bf16(rw_ref[...])
    logits = (jnp.dot(h_hi, w_hi, preferred_element_type=F32)
              + jnp.dot(h_hi, w_lo, preferred_element_type=F32)
              + jnp.dot(h_lo, w_hi, preferred_element_type=F32))
    lane = lax.broadcasted_iota(I32, logits.shape, 1)
    logits = jnp.where(lane < N_EXPERTS, logits, NEG_BIG)
    v1 = jnp.max(logits, axis=-1, keepdims=True)
    i1 = jnp.min(jnp.where(logits == v1, lane, LANES), axis=-1, keepdims=True)
    rest = jnp.where(lane == i1, NEG_BIG, logits)
    v2 = jnp.max(rest, axis=-1, keepdims=True)
    i2 = jnp.min(jnp.where(rest == v2, lane, LANES), axis=-1, keepdims=True)
    e = jnp.exp(v2 - v1)
    g1 = 1.0 / (1.0 + e)
    g2 = e / (1.0 + e)
    idx_ref[...] = jnp.where(lane == 0, i1, jnp.where(lane == 1, i2, 0))
    gate_ref[...] = jnp.where(lane == 0, g1, jnp.where(lane == 1, g2, 0.0))


def _route(x, g, rw_padded):
    m, d = x.shape
    bm = 256
    return pl.pallas_call(
        _route_kernel,
        grid=(m // bm,),
        in_specs=[
            pl.BlockSpec((bm, d), lambda i: (i, 0)),
            pl.BlockSpec((1, d), lambda i: (0, 0)),
            pl.BlockSpec((d, LANES), lambda i: (0, 0)),
        ],
        out_specs=[
            pl.BlockSpec((bm, d), lambda i: (i, 0)),
            pl.BlockSpec((bm, LANES), lambda i: (i, 0)),
            pl.BlockSpec((bm, LANES), lambda i: (i, 0)),
        ],
        out_shape=[
            jax.ShapeDtypeStruct((m, d), F32),
            jax.ShapeDtypeStruct((m, LANES), I32),
            jax.ShapeDtypeStruct((m, LANES), F32),
        ],
        compiler_params=_cparams(("arbitrary",)),
        name="ffn_norm_route",
    )(x, g, rw_padded)


def _swiglu_kernel(eid_ref, nsub_ref, blk_ref, tot_ref, x_ref, w1_ref, w3_ref, w2_ref, *rest, has_res):
    if has_res:
        res_ref, o_ref, w1s, w3s, w2s = rest
    else:
        o_ref, w1s, w3s, w2s = rest
    t = pl.program_id(0)
    j = pl.program_id(1)
    nsub = nsub_ref[t]

    @pl.when(j == 0)
    def _():
        def init(i, carry):
            rows = pl.ds(pl.multiple_of(i * FFN_SUB, FFN_SUB), FFN_SUB)
            if has_res:
                o_ref[rows, :] = res_ref[rows, :]
            else:
                o_ref[rows, :] = jnp.zeros((FFN_SUB, o_ref.shape[1]), F32)
            return carry
        lax.fori_loop(0, o_ref.shape[0] // FFN_SUB, init, 0)

    @pl.when(nsub > 0)
    def _():
        w1s[...] = w1_ref[0].astype(BF16)
        w3s[...] = w3_ref[0].astype(BF16)
        w2s[...] = w2_ref[0].astype(BF16)

        def body(i, carry):
            rows = pl.ds(pl.multiple_of(i * FFN_SUB, FFN_SUB), FFN_SUB)
            xs = x_ref[rows, :]
            a = jnp.dot(xs, w1s[...], preferred_element_type=F32)
            b = jnp.dot(xs, w3s[...], preferred_element_type=F32)
            hidden = (a * jax.nn.sigmoid(a) * b).astype(BF16)
            o_ref[rows, :] += jnp.dot(hidden, w2s[...], preferred_element_type=F32)
            return carry

        lax.fori_loop(0, nsub, body, 0)


def _swiglu(meta, x, w1, w3, w2, res, tile_rows):
    eid, nsub, blk, tot = meta
    n_tiles = eid.shape[0]
    d = x.shape[1]
    f = w1.shape[2]
    n_j = f // FFN_BF
    has_res = res is not None

    def row_map(t, j, eid, nsub, blk, tot):
        return (blk[t], 0)

    def out_map(t, j, eid, nsub, blk, tot):
        return (t, 0)

    def hidden_block(t, j, tot):
        return jnp.where(t < tot[0], j, n_j - 1)

    def w13_map(t, j, eid, nsub, blk, tot):
        return (eid[t], 0, hidden_block(t, j, tot))

    def w2_map(t, j, eid, nsub, blk, tot):
        return (eid[t], hidden_block(t, j, tot), 0)

    once = pl.Buffered(1)
    in_specs = [
        pl.BlockSpec((tile_rows, d), row_map, pipeline_mode=once),
        pl.BlockSpec((1, d, FFN_BF), w13_map),
        pl.BlockSpec((1, d, FFN_BF), w13_map),
        pl.BlockSpec((1, FFN_BF, d), w2_map),
    ]
    args = [x, w1, w3, w2]
    if has_res:
        in_specs.append(pl.BlockSpec((tile_rows, d), row_map, pipeline_mode=once))
        args.append(res)
    return pl.pallas_call(
        functools.partial(_swiglu_kernel, has_res=has_res),
        grid_spec=pltpu.PrefetchScalarGridSpec(
            num_scalar_prefetch=4,
            grid=(n_tiles, n_j),
            in_specs=in_specs,
            out_specs=pl.BlockSpec((tile_rows, d), out_map, pipeline_mode=once),
            scratch_shapes=[
                pltpu.VMEM((d, FFN_BF), BF16),
                pltpu.VMEM((d, FFN_BF), BF16),
                pltpu.VMEM((FFN_BF, d), BF16),
            ],
        ),
        out_shape=jax.ShapeDtypeStruct((n_tiles * tile_rows, d), F32),
        compiler_params=_cparams(("arbitrary", "arbitrary"), VMEM_BUDGET),
        name="swiglu_res" if has_res else "swiglu_moe",
    )(eid, nsub, blk, tot, *args)


def _gather_kernel(nvalid_ref, tok_ref, h_hbm, o_ref, buf, sem):
    i = pl.program_id(0)
    n = o_ref.shape[0]
    valid = nvalid_ref[i] > 0

    def row_copy(r):
        return pltpu.make_async_copy(h_hbm.at[pl.ds(tok_ref[0, 0, r], 1)], buf.at[pl.ds(r, 1)], sem)

    @pl.when(valid)
    def _():
        def start(r, c):
            row_copy(r).start()
            return c
        lax.fori_loop(0, n, start, 0)

        def wait(r, c):
            row_copy(r).wait()
            return c
        lax.fori_loop(0, n, wait, 0)
        o_ref[...] = buf[...].astype(BF16)

    @pl.when(jnp.logical_not(valid))
    def _():
        o_ref[...] = jnp.zeros(o_ref.shape, BF16)


def _gather_rows(nvalid_blocks, tok_of_row, h):
    n_rows = tok_of_row.shape[0]
    d = h.shape[1]
    nb = n_rows // GATHER_BM
    tok3 = tok_of_row.reshape(nb, 1, GATHER_BM)
    return pl.pallas_call(
        _gather_kernel,
        grid_spec=pltpu.PrefetchScalarGridSpec(
            num_scalar_prefetch=1,
            grid=(nb,),
            in_specs=[
                pl.BlockSpec((1, 1, GATHER_BM), lambda i, nv: (i, 0, 0), memory_space=pltpu.SMEM),
                pl.BlockSpec(memory_space=pl.ANY),
            ],
            out_specs=pl.BlockSpec((GATHER_BM, d), lambda i, nv: (i, 0)),
            scratch_shapes=[pltpu.VMEM((GATHER_BM, d), F32), pltpu.SemaphoreType.DMA(())],
        ),
        out_shape=jax.ShapeDtypeStruct((n_rows, d), BF16),
        compiler_params=_cparams(("arbitrary",)),
        name="moe_gather",
    )(nvalid_blocks, tok3, h)


def _combine_kernel(pos_ref, x_ref, gate_ref, y_hbm, o_ref, buf, sem):
    n = o_ref.shape[0]

    def row_copy(r, s):
        return pltpu.make_async_copy(y_hbm.at[pl.ds(pos_ref[0, s, r], 1)], buf.at[s, pl.ds(r, 1)], sem)

    def start(r, c):
        row_copy(r, 0).start()
        row_copy(r, 1).start()
        return c
    lax.fori_loop(0, n, start, 0)

    def wait(r, c):
        row_copy(r, 0).wait()
        row_copy(r, 1).wait()
        return c
    lax.fori_loop(0, n, wait, 0)
    g = gate_ref[...]
    o_ref[...] = x_ref[...] + g[:, 0:1] * buf[0] + g[:, 1:2] * buf[1]


def _combine(pos, x, gate, y):
    m, d = x.shape
    nb = m // GATHER_BM
    pos3 = pos.reshape(nb, GATHER_BM, TOP_K).transpose(0, 2, 1)
    return pl.pallas_call(
        _combine_kernel,
        grid=(nb,),
        in_specs=[
            pl.BlockSpec((1, TOP_K, GATHER_BM), lambda i: (i, 0, 0), memory_space=pltpu.SMEM),
            pl.BlockSpec((GATHER_BM, d), lambda i: (i, 0)),
            pl.BlockSpec((GATHER_BM, LANES), lambda i: (i, 0)),
            pl.BlockSpec(memory_space=pl.ANY),
        ],
        out_specs=pl.BlockSpec((GATHER_BM, d), lambda i: (i, 0)),
        out_shape=jax.ShapeDtypeStruct((m, d), F32),
        scratch_shapes=[pltpu.VMEM((TOP_K, GATHER_BM, d), F32), pltpu.SemaphoreType.DMA(())],
        compiler_params=_cparams(("arbitrary",)),
        name="moe_combine",
    )(pos3, x, gate, y)


def _dispatch_plan(idx, n_tokens):
    n_assign = n_tokens * TOP_K
    n_tiles = n_assign // MOE_TILE + N_EXPERTS
    e_flat = idx.reshape(n_assign)
    onehot = (e_flat[:, None] == jnp.arange(N_EXPERTS, dtype=I32)[None, :]).astype(I32)
    counts = jnp.sum(onehot, axis=0)
    rank = jnp.take_along_axis(jnp.cumsum(onehot, axis=0) - onehot, e_flat[:, None], axis=1)[:, 0]
    tiles_e = (counts + MOE_TILE - 1) // MOE_TILE
    tile_end = jnp.cumsum(tiles_e)
    tile_start = tile_end - tiles_e
    total = tile_end[-1]
    pos = tile_start[e_flat] * MOE_TILE + rank
    t = jnp.arange(n_tiles, dtype=I32)
    t_eff = jnp.minimum(t, total - 1)
    eid = jnp.minimum(jnp.sum((t_eff[:, None] >= tile_end[None, :]).astype(I32), axis=1), N_EXPERTS - 1)
    rows_valid = jnp.clip(counts[eid] - (t_eff - tile_start[eid]) * MOE_TILE, 0, MOE_TILE)
    rows_valid = jnp.where(t < total, rows_valid, 0)
    nsub = (rows_valid + FFN_SUB - 1) // FFN_SUB
    sub_per_tile = MOE_TILE // GATHER_BM
    sub_id = jnp.arange(n_tiles * sub_per_tile, dtype=I32)
    nvalid_blocks = ((sub_id % sub_per_tile) < nsub[sub_id // sub_per_tile]).astype(I32)
    tok_of_row = jnp.zeros((n_tiles * MOE_TILE,), I32).at[pos].set(jnp.arange(n_assign, dtype=I32) // TOP_K)
    meta = (eid, nsub.astype(I32), t_eff, jnp.reshape(total, (1,)).astype(I32))
    return meta, nvalid_blocks, tok_of_row, pos.reshape(n_tokens, TOP_K)


def _moe(x, g, rw, w1_all, w3_all, w2_all, layer_idx):
    m, d = x.shape
    w1 = w1_all.reshape((-1,) + w1_all.shape[2:])
    w3 = w3_all.reshape((-1,) + w3_all.shape[2:])
    w2 = w2_all.reshape((-1,) + w2_all.shape[2:])
    rw_padded = jnp.zeros((d, LANES), F32).at[:, :N_EXPERTS].set(rw)
    h, idx, gate = _route(x, g, rw_padded)
    meta, nvalid_blocks, tok_of_row, pos = _dispatch_plan(idx[:, :TOP_K], m)
    meta = (meta[0] + layer_idx * N_EXPERTS,) + meta[1:]
    xs = _gather_rows(nvalid_blocks, tok_of_row, h)
    y = _swiglu(meta, xs, w1, w3, w2, None, MOE_TILE)
    return _combine(pos, x, gate, y)


def _dense(x, g, w1_all, w3_all, w2_all, layer_idx):
    m, d = x.shape
    h = _ffn_norm(x, g)
    n_tiles = m // DENSE_TILE
    meta = (jnp.full((n_tiles,), layer_idx, I32), jnp.full((n_tiles,), DENSE_TILE // FFN_SUB, I32),
            jnp.arange(n_tiles, dtype=I32), jnp.full((1,), n_tiles, I32))
    return _swiglu(meta, h, w1_all, w3_all, w2_all, x, DENSE_TILE)


def kernel(x, positions, attn_norm, w_in, q_norm, k_norm, lambda_q1, lambda_k1, lambda_q2, lambda_k2, subln, pool_w, pool_b, pool_scale, w_out, ffn_norm, dense_w1, dense_w3, dense_w2, router_w, moe_w1, moe_w3, moe_w2):
    batch, seq, d = x.shape
    depth = w_in.shape[0]
    m = batch * seq
    xf = x.reshape(m, d)
    pos = positions.reshape(m, 1)
    inv_freq = 1.0 / (ROPE_THETA ** (jnp.arange(0, HEAD_DIM, 2, dtype=F32) / HEAD_DIM))
    invf = jnp.tile(inv_freq, LANES // (HEAD_DIM // 2))[None, :]
    for l in range(depth):
        qk, v, u = _in_proj(xf, attn_norm[l][None, :], w_in, l)
        gain = jnp.concatenate([jnp.tile(q_norm[l], ATTN_WIDTH // HEAD_DIM),
                                jnp.tile(k_norm[l], ATTN_WIDTH // HEAD_DIM)])[None, :]
        qk = _qk_rope(pos, invf, qk, gain)
        lambda_init = 0.8 - 0.6 * math.exp(-0.3 * l)
        a = _attention(jnp.full((1,), lambda_init, F32), qk, v, lambda_q1[l][None, :], lambda_k1[l][None, :],
                       lambda_q2[l][None, :], lambda_k2[l][None, :], subln[l][:, None], batch, seq)
        p = _pool(u, pool_w, l, pool_b[l].reshape(1, POOL_WIDTH), pool_scale[l][None, :], seq)
        xf = _out_proj(a, p, w_out, l, xf)
        i = l // 2
        if l % 2 == 0:
            xf = _dense(xf, ffn_norm[l][None, :], dense_w1, dense_w3, dense_w2, i)
        else:
            xf = _moe(xf, ffn_norm[l][None, :], router_w[i], moe_w1, moe_w3, moe_w2, i)
    return xf.reshape(batch, seq, d)
```

```python
import functools
import math

import jax
import jax.numpy as jnp
from jax import lax
from jax.experimental import pallas as pl
from jax.experimental.pallas import tpu as pltpu

F32 = jnp.float32
BF16 = jnp.bfloat16
I32 = jnp.int32

D_MODEL = 2048
ATTN_WIDTH = 1024
POOL_WIDTH = 1024
HEAD_DIM = 64
N_HEADS = 8
HEAD_WIDTH = 2 * HEAD_DIM
POOL_WINDOWS = (2, 4, 8, 16)
POOL_GROUP_DIM = 256
ROPE_THETA = 10000.0
NORM_EPS = 1e-6
N_EXPERTS = 8
TOP_K = 2
NEG_BIG = -1e30
LOG2_E = math.log2(math.e)

LANES = 128
VMEM_BUDGET = 56 * 1024 * 1024

PROJ_BM = 1024
PROJ_BN = 1024
IN_PROJ_BN = 512
NORM_CHUNK = 128
ROPE_BM = 256
ATTN_BQ = 512
POOL_BM = 256
FFN_SUB = 256
FFN_BF = 256
DENSE_TILE = 1024
MOE_TILE = 2560
GATHER_BM = 256
DMA_UNROLL = 8


def _cparams(sem, vmem=None):
    return pltpu.CompilerParams(dimension_semantics=sem, vmem_limit_bytes=vmem)


def _split_bf16(x):
    hi = x.astype(BF16)
    lo = (x - hi.astype(F32)).astype(BF16)
    return hi, lo


def _rms_rows(x_ref, g_ref, h_ref, rows):
    def body(c, carry):
        r = pl.multiple_of(c * NORM_CHUNK, NORM_CHUNK)
        x = x_ref[pl.ds(r, NORM_CHUNK), :]
        ms = jnp.mean(x * x, axis=-1, keepdims=True)
        h_ref[pl.ds(r, NORM_CHUNK), :] = (x * lax.rsqrt(ms + NORM_EPS) * g_ref[...]).astype(h_ref.dtype)
        return carry
    lax.fori_loop(0, rows // NORM_CHUNK, body, 0)


def _in_proj_kernel(x_ref, g_ref, w_ref, qk_ref, v_ref, u_ref, h_ref, *, n_qk, n_v):
    j = pl.program_id(1)

    @pl.when(j == 0)
    def _():
        _rms_rows(x_ref, g_ref, h_ref, PROJ_BM)

    y = jnp.dot(h_ref[...], w_ref[...].astype(BF16), preferred_element_type=F32)

    @pl.when(j < n_qk)
    def _():
        qk_ref[...] = y

    @pl.when((j >= n_qk) & (j < n_qk + n_v))
    def _():
        v_ref[...] = y.astype(BF16)

    @pl.when(j >= n_qk + n_v)
    def _():
        u_ref[...] = y


def _in_proj(x, g, w_all, layer):
    m, d = x.shape
    bn = IN_PROJ_BN
    n_qk = 2 * ATTN_WIDTH // bn
    n_v = ATTN_WIDTH // bn
    n_u = POOL_WIDTH // bn
    assert w_all.shape[2] == (n_qk + n_v + n_u) * bn and m % PROJ_BM == 0
    return pl.pallas_call(
        functools.partial(_in_proj_kernel, n_qk=n_qk, n_v=n_v),
        grid=(m // PROJ_BM, n_qk + n_v + n_u),
        in_specs=[
            pl.BlockSpec((PROJ_BM, d), lambda i, j: (i, 0)),
            pl.BlockSpec((1, d), lambda i, j: (0, 0)),
            pl.BlockSpec((None, d, bn), lambda i, j: (layer, 0, j)),
        ],
        out_specs=[
            pl.BlockSpec((PROJ_BM, bn), lambda i, j: (i, jnp.minimum(j, n_qk - 1))),
            pl.BlockSpec((PROJ_BM, bn), lambda i, j: (i, jnp.clip(j - n_qk, 0, n_v - 1))),
            pl.BlockSpec((PROJ_BM, bn), lambda i, j: (i, jnp.clip(j - n_qk - n_v, 0, n_u - 1))),
        ],
        out_shape=[
            jax.ShapeDtypeStruct((m, 2 * ATTN_WIDTH), F32),
            jax.ShapeDtypeStruct((m, ATTN_WIDTH), BF16),
            jax.ShapeDtypeStruct((m, POOL_WIDTH), F32),
        ],
        scratch_shapes=[pltpu.VMEM((PROJ_BM, d), BF16)],
        compiler_params=_cparams(("arbitrary", "arbitrary"), VMEM_BUDGET),
        name="in_proj",
    )(x, g, w_all)


def _qk_rope_kernel(pos_ref, invf_ref, qk_ref, gain_ref, o_ref):
    bm = qk_ref.shape[0]
    ang = pos_ref[...].astype(F32) * invf_ref[...]
    cos = jnp.cos(ang)
    sin = jnp.sin(ang)
    lane = lax.broadcasted_iota(I32, (bm, LANES), 1)
    first_half = (lane % HEAD_DIM) < (HEAD_DIM // 2)
    sin_signed = jnp.where(first_half, -sin, sin)
    gr = lax.broadcasted_iota(I32, (LANES, LANES), 0) // HEAD_DIM
    gc = lax.broadcasted_iota(I32, (LANES, LANES), 1) // HEAD_DIM
    group_ones = jnp.where(gr == gc, 1.0, 0.0).astype(BF16)
    n_blocks = qk_ref.shape[1] // LANES
    for hb in range(n_blocks):
        cols = slice(hb * LANES, (hb + 1) * LANES)
        x = qk_ref[:, cols]
        hi, lo = _split_bf16(x * x)
        ssum = (jnp.dot(hi, group_ones, preferred_element_type=F32)
                + jnp.dot(lo, group_ones, preferred_element_type=F32))
        y = x * lax.rsqrt(ssum * (1.0 / HEAD_DIM) + NORM_EPS) * gain_ref[:, cols]
        swapped = jnp.where(first_half, pltpu.roll(y, LANES - HEAD_DIM // 2, 1),
                            pltpu.roll(y, HEAD_DIM // 2, 1))
        r = y * cos + swapped * sin_signed
        if hb < n_blocks // 2:
            r = r * (HEAD_DIM ** -0.5 * LOG2_E)
        o_ref[:, cols] = r.astype(BF16)


def _qk_rope(pos, invf, qk, gain):
    m, w = qk.shape
    return pl.pallas_call(
        _qk_rope_kernel,
        grid=(m // ROPE_BM,),
        in_specs=[
            pl.BlockSpec((ROPE_BM, 1), lambda i: (i, 0)),
            pl.BlockSpec((1, LANES), lambda i: (0, 0)),
            pl.BlockSpec((ROPE_BM, w), lambda i: (i, 0)),
            pl.BlockSpec((1, w), lambda i: (0, 0)),
        ],
        out_specs=pl.BlockSpec((ROPE_BM, w), lambda i: (i, 0)),
        out_shape=jax.ShapeDtypeStruct((m, w), BF16),
        compiler_params=_cparams(("arbitrary",)),
        name="qk_rope",
    )(pos, invf, qk, gain)


def _block_streams(nq):
    chains = [[(qi, kb) for kb in range(qi + 1)] for qi in range(nq - 1, -1, -1)]
    streams = ([], [])
    for chain in chains:
        min(streams, key=len).extend(chain)
    order = []
    for i in range(max(len(s) for s in streams)):
        order.extend(s[i] for s in streams if i < len(s))
    return order


def _attn_kernel(linit_ref, q_ref, k_ref, v_ref, lq1_ref, lk1_ref, lq2_ref, lk2_ref, sg_ref, o_ref,
                 q2_ref, vt_ref, m_ref, l_ref, acc_ref):
    nq, _, bq = vt_ref.shape
    lane = lax.broadcasted_iota(I32, (bq, HEAD_WIDTH), 1)
    for c in range(nq):
        rows = slice(c * bq, (c + 1) * bq)
        vt_ref[c] = v_ref[rows, :].astype(F32).T.astype(BF16)
        q = q_ref[rows, :]
        zero = jnp.zeros_like(q)
        q2_ref[c, 0:bq, :] = jnp.where(lane < HEAD_DIM, q, zero)
        q2_ref[c, bq:2 * bq, :] = jnp.where(lane >= HEAD_DIM, q, zero)
    m_ref[...] = jnp.full(m_ref.shape, NEG_BIG, F32)
    l_ref[...] = jnp.zeros(l_ref.shape, F32)
    acc_ref[...] = jnp.zeros(acc_ref.shape, F32)

    for qi, kb in _block_streams(nq):
        k = k_ref[kb * bq:(kb + 1) * bq, :]
        s = lax.dot_general(k, q2_ref[qi], (((1,), (1,)), ((), ())), preferred_element_type=F32)
        if kb == qi:
            key = lax.broadcasted_iota(I32, s.shape, 0)
            qry = lax.broadcasted_iota(I32, s.shape, 1) & (bq - 1)
            s = jnp.where(key <= qry, s, NEG_BIG)
        m_old = m_ref[qi]
        m_new = jnp.maximum(m_old, jnp.max(s, axis=0, keepdims=True))
        alpha = jnp.exp2(m_old - m_new)
        p = jnp.exp2(s - m_new)
        l_ref[qi] = alpha * l_ref[qi] + jnp.sum(p, axis=0, keepdims=True)
        acc_ref[qi] = alpha * acc_ref[qi] + jnp.dot(vt_ref[kb], p.astype(BF16), preferred_element_type=F32)
        m_ref[qi] = m_new

    lambda_init = linit_ref[0]
    lam = (jnp.exp(jnp.sum(lq1_ref[...] * lk1_ref[...], axis=-1, keepdims=True))
           - jnp.exp(jnp.sum(lq2_ref[...] * lk2_ref[...], axis=-1, keepdims=True))
           + lambda_init)
    for qi in range(nq):
        acc = acc_ref[qi]
        l = l_ref[qi]
        o = acc[:, :bq] / l[:, :bq] - lam * (acc[:, bq:] / l[:, bq:])
        ms = jnp.mean(o * o, axis=0, keepdims=True)
        o = o * lax.rsqrt(ms + NORM_EPS) * sg_ref[...] * (1.0 - lambda_init)
        o_ref[qi * bq:(qi + 1) * bq, :] = o.T.astype(BF16)


def _attention(linit, qk, v, lq1, lk1, lq2, lk2, sg_col, batch, seq):
    m = qk.shape[0]
    nq = seq // ATTN_BQ
    vec = pl.BlockSpec((1, HEAD_DIM), lambda b, h: (0, 0))
    return pl.pallas_call(
        _attn_kernel,
        grid=(batch, N_HEADS),
        in_specs=[
            pl.BlockSpec(memory_space=pltpu.SMEM),
            pl.BlockSpec((seq, HEAD_WIDTH), lambda b, h: (b, h)),
            pl.BlockSpec((seq, HEAD_WIDTH), lambda b, h: (b, N_HEADS + h)),
            pl.BlockSpec((seq, HEAD_WIDTH), lambda b, h: (b, h)),
            vec, vec, vec, vec,
            pl.BlockSpec((HEAD_WIDTH, 1), lambda b, h: (0, 0)),
        ],
        out_specs=pl.BlockSpec((seq, HEAD_WIDTH), lambda b, h: (b, h)),
        out_shape=jax.ShapeDtypeStruct((m, ATTN_WIDTH), BF16),
        scratch_shapes=[
            pltpu.VMEM((nq, 2 * ATTN_BQ, HEAD_WIDTH), BF16),
            pltpu.VMEM((nq, HEAD_WIDTH, ATTN_BQ), BF16),
            pltpu.VMEM((nq, 1, 2 * ATTN_BQ), F32),
            pltpu.VMEM((nq, 1, 2 * ATTN_BQ), F32),
            pltpu.VMEM((nq, HEAD_WIDTH, 2 * ATTN_BQ), F32),
        ],
        compiler_params=_cparams(("arbitrary", "arbitrary")),
        name="diff_attn",
    )(linit, qk, qk, v, lq1, lk1, lq2, lk2, sg_col)


def _pool_kernel(uc_ref, up_ref, w_ref, b_ref, sc_ref, o_ref, *, chunks_per_seq):
    c = pl.program_id(0) % chunks_per_seq
    bm = uc_ref.shape[0]
    row = lax.broadcasted_iota(I32, (bm, bm), 0)
    col = lax.broadcasted_iota(I32, (bm, bm), 1)
    t = c * bm + lax.broadcasted_iota(I32, (bm, 1), 0)
    has_prev = c > 0
    for g, win in enumerate(POOL_WINDOWS):
        cols = slice(g * POOL_GROUP_DIM, (g + 1) * POOL_GROUP_DIM)
        cur = jnp.where((row >= col) & (row - col < win), 1.0, 0.0).astype(BF16)
        prv = jnp.where(col - row > bm - win, 1.0, 0.0).astype(BF16)
        u = uc_ref[:, cols]
        u_hi, u_lo = _split_bf16(u)
        p_hi, p_lo = _split_bf16(up_ref[:, cols])
        wsum = (jnp.dot(cur, u_hi, preferred_element_type=F32)
                + jnp.dot(cur, u_lo, preferred_element_type=F32))
        wprev = (jnp.dot(prv, p_hi, preferred_element_type=F32)
                 + jnp.dot(prv, p_lo, preferred_element_type=F32))
        wsum = wsum + jnp.where(has_prev, wprev, 0.0)
        cnt = jnp.minimum(t + 1, win).astype(F32)
        d = wsum / cnt - u
        y = jnp.dot(d.astype(BF16), w_ref[g].astype(BF16), preferred_element_type=F32) + b_ref[:, cols]
        o_ref[:, cols] = (y * sc_ref[:, cols]).astype(BF16)


def _pool(u, w_all, layer, b, sc, seq):
    m, width = u.shape
    cps = seq // POOL_BM
    return pl.pallas_call(
        functools.partial(_pool_kernel, chunks_per_seq=cps),
        grid=(m // POOL_BM,),
        in_specs=[
            pl.BlockSpec((POOL_BM, width), lambda i: (i, 0)),
            pl.BlockSpec((POOL_BM, width), lambda i: (jnp.maximum(i - 1, 0), 0)),
            pl.BlockSpec((None,) + w_all.shape[1:], lambda i: (layer, 0, 0, 0)),
            pl.BlockSpec((1, width), lambda i: (0, 0)),
            pl.BlockSpec((1, width), lambda i: (0, 0)),
        ],
        out_specs=pl.BlockSpec((POOL_BM, width), lambda i: (i, 0)),
        out_shape=jax.ShapeDtypeStruct((m, width), BF16),
        compiler_params=_cparams(("arbitrary",)),
        name="pool_mixer",
    )(u, u, w_all, b, sc)


def _out_proj_kernel(a_ref, p_ref, wa_ref, wp_ref, x_ref, o_ref):
    y = jnp.dot(a_ref[...], wa_ref[...].astype(BF16), preferred_element_type=F32)
    y = y + jnp.dot(p_ref[...], wp_ref[...].astype(BF16), preferred_element_type=F32)
    o_ref[...] = x_ref[...] + y


def _out_proj(a, p, w_all, layer, x):
    m, d = x.shape
    ka = a.shape[1]
    kp = p.shape[1]
    assert ka == kp
    return pl.pallas_call(
        _out_proj_kernel,
        grid=(m // PROJ_BM, d // PROJ_BN),
        in_specs=[
            pl.BlockSpec((PROJ_BM, ka), lambda i, j: (i, 0)),
            pl.BlockSpec((PROJ_BM, kp), lambda i, j: (i, 0)),
            pl.BlockSpec((None, ka, PROJ_BN), lambda i, j: (layer, 0, j)),
            pl.BlockSpec((None, kp, PROJ_BN), lambda i, j: (layer, 1, j)),
            pl.BlockSpec((PROJ_BM, PROJ_BN), lambda i, j: (i, j)),
        ],
        out_specs=pl.BlockSpec((PROJ_BM, PROJ_BN), lambda i, j: (i, j)),
        out_shape=jax.ShapeDtypeStruct((m, d), F32),
        compiler_params=_cparams(("arbitrary", "arbitrary"), VMEM_BUDGET),
        name="out_proj",
    )(a, p, w_all, w_all, x)


def _ffn_norm_kernel(x_ref, g_ref, h_ref):
    _rms_rows(x_ref, g_ref, h_ref, x_ref.shape[0])


def _ffn_norm(x, g):
    m, d = x.shape
    bm = 512
    return pl.pallas_call(
        _ffn_norm_kernel,
        grid=(m // bm,),
        in_specs=[pl.BlockSpec((bm, d), lambda i: (i, 0)), pl.BlockSpec((1, d), lambda i: (0, 0))],
        out_specs=pl.BlockSpec((bm, d), lambda i: (i, 0)),
        out_shape=jax.ShapeDtypeStruct((m, d), BF16),
        compiler_params=_cparams(("arbitrary",)),
        name="ffn_norm",
    )(x, g)


def _route_kernel(x_ref, g_ref, rw_ref, h_ref, idx_ref, gate_ref):
    x = x_ref[...]
    ms = jnp.mean(x * x, axis=-1, keepdims=True)
    h = x * lax.rsqrt(ms + NORM_EPS) * g_ref[...]
    h_ref[...] = h
    h_hi, h_lo = _split_bf16(h)
    w_hi, w_lo = _split_bf16(rw_ref[...])
    logits = (jnp.dot(h_hi, w_hi, preferred_element_type=F32)
              + jnp.dot(h_hi, w_lo, preferred_element_type=F32)
              + jnp.dot(h_lo, w_hi, preferred_element_type=F32))
    lane = lax.broadcasted_iota(I32, logits.shape, 1)
    logits = jnp.where(lane < N_EXPERTS, logits, NEG_BIG)
    v1 = jnp.max(logits, axis=-1, keepdims=True)
    i1 = jnp.min(jnp.where(logits == v1, lane, LANES), axis=-1, keepdims=True)
    rest = jnp.where(lane == i1, NEG_BIG, logits)
    v2 = jnp.max(rest, axis=-1, keepdims=True)
    i2 = jnp.min(jnp.where(rest == v2, lane, LANES), axis=-1, keepdims=True)
    e = jnp.exp(v2 - v1)
    g1 = 1.0 / (1.0 + e)
    g2 = e / (1.0 + e)
    idx_ref[...] = jnp.where(lane == 0, i1, jnp.where(lane == 1, i2, 0))
    gate_ref[...] = jnp.where(lane == 0, g1, jnp.where(lane == 1, g2, 0.0))


def _route(x, g, rw_padded):
    m, d = x.shape
    bm = 256
    return pl.pallas_call(
        _route_kernel,
        grid=(m // bm,),
        in_specs=[
            pl.BlockSpec((bm, d), lambda i: (i, 0)),
            pl.BlockSpec((1, d), lambda i: (0, 0)),
            pl.BlockSpec((d, LANES), lambda i: (0, 0)),
        ],
        out_specs=[
            pl.BlockSpec((bm, d), lambda i: (i, 0)),
            pl.BlockSpec((bm, LANES), lambda i: (i, 0)),
            pl.BlockSpec((bm, LANES), lambda i: (i, 0)),
        ],
        out_shape=[
            jax.ShapeDtypeStruct((m, d), F32),
            jax.ShapeDtypeStruct((m, LANES), I32),
            jax.ShapeDtypeStruct((m, LANES), F32),
        ],
        compiler_params=_cparams(("arbitrary",)),
        name="ffn_norm_route",
    )(x, g, rw_padded)


def _swiglu_kernel(eid_ref, nsub_ref, blk_ref, tot_ref, x_ref, w1_ref, w3_ref, w2_ref, *rest, has_res):
    if has_res:
        res_ref, o_ref, w1s, w3s, w2s = rest
    else:
        o_ref, w1s, w3s, w2s = rest
    t = pl.program_id(0)
    j = pl.program_id(1)
    nsub = nsub_ref[t]

    @pl.when(j == 0)
    def _():
        def init(i, carry):
            rows = pl.ds(pl.multiple_of(i * FFN_SUB, FFN_SUB), FFN_SUB)
            if has_res:
                o_ref[rows, :] = res_ref[rows, :]
            else:
                o_ref[rows, :] = jnp.zeros((FFN_SUB, o_ref.shape[1]), F32)
            return carry
        lax.fori_loop(0, o_ref.shape[0] // FFN_SUB, init, 0)

    def sub_block(i):
        rows = pl.ds(pl.multiple_of(i * FFN_SUB, FFN_SUB), FFN_SUB)
        xs = x_ref[rows, :]
        a = jnp.dot(xs, w1s[...], preferred_element_type=F32)
        b = jnp.dot(xs, w3s[...], preferred_element_type=F32)
        hidden = (a * jax.nn.sigmoid(a) * b).astype(BF16)
        o_ref[rows, :] += jnp.dot(hidden, w2s[...], preferred_element_type=F32)

    def cast_weights():
        w1s[...] = w1_ref[0].astype(BF16)
        w3s[...] = w3_ref[0].astype(BF16)
        w2s[...] = w2_ref[0].astype(BF16)

    if has_res:
        cast_weights()
        for i in range(o_ref.shape[0] // FFN_SUB):
            sub_block(i)
    else:
        @pl.when(nsub > 0)
        def _():
            cast_weights()

            def pair(i, carry):
                sub_block(2 * i)
                sub_block(2 * i + 1)
                return carry
            lax.fori_loop(0, nsub // 2, pair, 0)

            @pl.when(nsub % 2 == 1)
            def _():
                sub_block(nsub - 1)


def _swiglu(meta, x, w1, w3, w2, res, tile_rows):
    eid, nsub, blk, tot = meta
    n_tiles = eid.shape[0]
    d = x.shape[1]
    f = w1.shape[2]
    n_j = f // FFN_BF
    has_res = res is not None

    def row_map(t, j, eid, nsub, blk, tot):
        return (blk[t], 0)

    def out_map(t, j, eid, nsub, blk, tot):
        return (t, 0)

    def hidden_block(t, j, tot):
        return jnp.where(t < tot[0], j, n_j - 1)

    def w13_map(t, j, eid, nsub, blk, tot):
        return (eid[t], 0, hidden_block(t, j, tot))

    def w2_map(t, j, eid, nsub, blk, tot):
        return (eid[t], hidden_block(t, j, tot), 0)

    once = pl.Buffered(1)
    in_specs = [
        pl.BlockSpec((tile_rows, d), row_map, pipeline_mode=once),
        pl.BlockSpec((1, d, FFN_BF), w13_map),
        pl.BlockSpec((1, d, FFN_BF), w13_map),
        pl.BlockSpec((1, FFN_BF, d), w2_map),
    ]
    args = [x, w1, w3, w2]
    if has_res:
        in_specs.append(pl.BlockSpec((tile_rows, d), row_map, pipeline_mode=once))
        args.append(res)
    return pl.pallas_call(
        functools.partial(_swiglu_kernel, has_res=has_res),
        grid_spec=pltpu.PrefetchScalarGridSpec(
            num_scalar_prefetch=4,
            grid=(n_tiles, n_j),
            in_specs=in_specs,
            out_specs=pl.BlockSpec((tile_rows, d), out_map, pipeline_mode=once),
            scratch_shapes=[
                pltpu.VMEM((d, FFN_BF), BF16),
                pltpu.VMEM((d, FFN_BF), BF16),
                pltpu.VMEM((FFN_BF, d), BF16),
            ],
        ),
        out_shape=jax.ShapeDtypeStruct((n_tiles * tile_rows, d), F32),
        compiler_params=_cparams(("arbitrary", "arbitrary"), VMEM_BUDGET),
        name="swiglu_res" if has_res else "swiglu_moe",
    )(eid, nsub, blk, tot, *args)


def _gather_kernel(nvalid_ref, tok_ref, h_hbm, o_ref, buf, sem):
    i = pl.program_id(0)
    n = o_ref.shape[0]
    valid = nvalid_ref[i] > 0

    def row_copy(r):
        return pltpu.make_async_copy(h_hbm.at[pl.ds(tok_ref[0, 0, r], 1)], buf.at[pl.ds(r, 1)], sem)

    @pl.when(valid)
    def _():
        def start(r, c):
            row_copy(r).start()
            return c
        lax.fori_loop(0, n, start, 0, unroll=DMA_UNROLL)

        def wait(r, c):
            row_copy(r).wait()
            return c
        lax.fori_loop(0, n, wait, 0, unroll=DMA_UNROLL)
        o_ref[...] = buf[...].astype(BF16)

    @pl.when(jnp.logical_not(valid))
    def _():
        o_ref[...] = jnp.zeros(o_ref.shape, BF16)


def _gather_rows(nvalid_blocks, tok_of_row, h):
    n_rows = tok_of_row.shape[0]
    d = h.shape[1]
    nb = n_rows // GATHER_BM
    tok3 = tok_of_row.reshape(nb, 1, GATHER_BM)
    return pl.pallas_call(
        _gather_kernel,
        grid_spec=pltpu.PrefetchScalarGridSpec(
            num_scalar_prefetch=1,
            grid=(nb,),
            in_specs=[
                pl.BlockSpec((1, 1, GATHER_BM), lambda i, nv: (i, 0, 0), memory_space=pltpu.SMEM),
                pl.BlockSpec(memory_space=pl.ANY),
            ],
            out_specs=pl.BlockSpec((GATHER_BM, d), lambda i, nv: (i, 0)),
            scratch_shapes=[pltpu.VMEM((GATHER_BM, d), F32), pltpu.SemaphoreType.DMA(())],
        ),
        out_shape=jax.ShapeDtypeStruct((n_rows, d), BF16),
        compiler_params=_cparams(("arbitrary",)),
        name="moe_gather",
    )(nvalid_blocks, tok3, h)


def _combine_kernel(pos_ref, x_ref, gate_ref, y_hbm, o_ref, buf, sem):
    n = o_ref.shape[0]

    def row_copy(r, s):
        return pltpu.make_async_copy(y_hbm.at[pl.ds(pos_ref[0, s, r], 1)], buf.at[s, pl.ds(r, 1)], sem)

    def start(r, c):
        row_copy(r, 0).start()
        row_copy(r, 1).start()
        return c
    lax.fori_loop(0, n, start, 0, unroll=DMA_UNROLL)

    def wait(r, c):
        row_copy(r, 0).wait()
        row_copy(r, 1).wait()
        return c
    lax.fori_loop(0, n, wait, 0, unroll=DMA_UNROLL)
    g = gate_ref[...]
    o_ref[...] = x_ref[...] + g[:, 0:1] * buf[0] + g[:, 1:2] * buf[1]


def _combine(pos, x, gate, y):
    m, d = x.shape
    nb = m // GATHER_BM
    pos3 = pos.reshape(nb, GATHER_BM, TOP_K).transpose(0, 2, 1)
    return pl.pallas_call(
        _combine_kernel,
        grid=(nb,),
        in_specs=[
            pl.BlockSpec((1, TOP_K, GATHER_BM), lambda i: (i, 0, 0), memory_space=pltpu.SMEM),
            pl.BlockSpec((GATHER_BM, d), lambda i: (i, 0)),
            pl.BlockSpec((GATHER_BM, LANES), lambda i: (i, 0)),
            pl.BlockSpec(memory_space=pl.ANY),
        ],
        out_specs=pl.BlockSpec((GATHER_BM, d), lambda i: (i, 0)),
        out_shape=jax.ShapeDtypeStruct((m, d), F32),
        scratch_shapes=[pltpu.VMEM((TOP_K, GATHER_BM, d), F32), pltpu.SemaphoreType.DMA(())],
        compiler_params=_cparams(("arbitrary",)),
        name="moe_combine",
    )(pos3, x, gate, y)


def _dispatch_plan(idx, n_tokens):
    n_assign = n_tokens * TOP_K
    n_tiles = n_assign // MOE_TILE + N_EXPERTS
    e_flat = idx.reshape(n_assign)
    onehot = (e_flat[:, None] == jnp.arange(N_EXPERTS, dtype=I32)[None, :]).astype(I32)
    counts = jnp.sum(onehot, axis=0)
    rank = jnp.take_along_axis(jnp.cumsum(onehot, axis=0) - onehot, e_flat[:, None], axis=1)[:, 0]
    tiles_e = (counts + MOE_TILE - 1) // MOE_TILE
    tile_end = jnp.cumsum(tiles_e)
    tile_start = tile_end - tiles_e
    total = tile_end[-1]
    pos = tile_start[e_flat] * MOE_TILE + rank
    t = jnp.arange(n_tiles, dtype=I32)
    t_eff = jnp.minimum(t, total - 1)
    eid = jnp.minimum(jnp.sum((t_eff[:, None] >= tile_end[None, :]).astype(I32), axis=1), N_EXPERTS - 1)
    rows_valid = jnp.clip(counts[eid] - (t_eff - tile_start[eid]) * MOE_TILE, 0, MOE_TILE)
    rows_valid = jnp.where(t < total, rows_valid, 0)
    nsub = (rows_valid + FFN_SUB - 1) // FFN_SUB
    sub_per_tile = MOE_TILE // GATHER_BM
    sub_id = jnp.arange(n_tiles * sub_per_tile, dtype=I32)
    nvalid_blocks = ((sub_id % sub_per_tile) < nsub[sub_id // sub_per_tile]).astype(I32)
    tok_of_row = jnp.zeros((n_tiles * MOE_TILE,), I32).at[pos].set(jnp.arange(n_assign, dtype=I32) // TOP_K)
    meta = (eid, nsub.astype(I32), t_eff, jnp.reshape(total, (1,)).astype(I32))
    return meta, nvalid_blocks, tok_of_row, pos.reshape(n_tokens, TOP_K)


def _moe(x, g, rw, w1_all, w3_all, w2_all, layer_idx):
    m, d = x.shape
    w1 = w1_all.reshape((-1,) + w1_all.shape[2:])
    w3 = w3_all.reshape((-1,) + w3_all.shape[2:])
    w2 = w2_all.reshape((-1,) + w2_all.shape[2:])
    rw_padded = jnp.zeros((d, LANES), F32).at[:, :N_EXPERTS].set(rw)
    h, idx, gate = _route(x, g, rw_padded)
    meta, nvalid_blocks, tok_of_row, pos = _dispatch_plan(idx[:, :TOP_K], m)
    meta = (meta[0] + layer_idx * N_EXPERTS,) + meta[1:]
    xs = _gather_rows(nvalid_blocks, tok_of_row, h)
    y = _swiglu(meta, xs, w1, w3, w2, None, MOE_TILE)
    return _combine(pos, x, gate, y)


def _dense(x, g, w1_all, w3_all, w2_all, layer_idx):
    m, d = x.shape
    h = _ffn_norm(x, g)
    n_tiles = m // DENSE_TILE
    meta = (jnp.full((n_tiles,), layer_idx, I32), jnp.full((n_tiles,), DENSE_TILE // FFN_SUB, I32),
            jnp.arange(n_tiles, dtype=I32), jnp.full((1,), n_tiles, I32))
    return _swiglu(meta, h, w1_all, w3_all, w2_all, x, DENSE_TILE)


def kernel(x, positions, attn_norm, w_in, q_norm, k_norm, lambda_q1, lambda_k1, lambda_q2, lambda_k2, subln, pool_w, pool_b, pool_scale, w_out, ffn_norm, dense_w1, dense_w3, dense_w2, router_w, moe_w1, moe_w3, moe_w2):
    batch, seq, d = x.shape
    depth = w_in.shape[0]
    m = batch * seq
    xf = x.reshape(m, d)
    pos = positions.reshape(m, 1)
    inv_freq = 1.0 / (ROPE_THETA ** (jnp.arange(0, HEAD_DIM, 2, dtype=F32) / HEAD_DIM))
    invf = jnp.tile(inv_freq, LANES // (HEAD_DIM // 2))[None, :]
    for l in range(depth):
        qk, v, u = _in_proj(xf, attn_norm[l][None, :], w_in, l)
        gain = jnp.concatenate([jnp.tile(q_norm[l], ATTN_WIDTH // HEAD_DIM),
                                jnp.tile(k_norm[l], ATTN_WIDTH // HEAD_DIM)])[None, :]
        qk = _qk_rope(pos, invf, qk, gain)
        lambda_init = 0.8 - 0.6 * math.exp(-0.3 * l)
        a = _attention(jnp.full((1,), lambda_init, F32), qk, v, lambda_q1[l][None, :], lambda_k1[l][None, :],
                       lambda_q2[l][None, :], lambda_k2[l][None, :], subln[l][:, None], batch, seq)
        p = _pool(u, pool_w, l, pool_b[l].reshape(1, POOL_WIDTH), pool_scale[l][None, :], seq)
        xf = _out_proj(a, p, w_out, l, xf)
        i = l // 2
        if l % 2 == 0:
            xf = _dense(xf, ffn_norm[l][None, :], dense_w1, dense_w3, dense_w2, i)
        else:
            xf = _moe(xf, ffn_norm[l][None, :], router_w[i], moe_w1, moe_w3, moe_w2, i)
    return xf.reshape(batch, seq, d)
```

```python
import functools
import math

import jax
import jax.numpy as jnp
from jax import lax
from jax.experimental import pallas as pl
from jax.experimental.pallas import tpu as pltpu

F32 = jnp.float32
BF16 = jnp.bfloat16
I32 = jnp.int32

D_MODEL = 2048
ATTN_WIDTH = 1024
POOL_WIDTH = 1024
HEAD_DIM = 64
N_HEADS = 8
HEAD_WIDTH = 2 * HEAD_DIM
POOL_WINDOWS = (2, 4, 8, 16)
POOL_GROUP_DIM = 256
ROPE_THETA = 10000.0
NORM_EPS = 1e-6
N_EXPERTS = 8
TOP_K = 2
NEG_BIG = -1e30
LOG2_E = math.log2(math.e)

LANES = 128
VMEM_BUDGET = 56 * 1024 * 1024

PROJ_BM = 1024
PROJ_BN = 1024
IN_PROJ_BN = 512
NORM_CHUNK = 128
ROPE_BM = 256
ATTN_BQ = 512
ATTN_STREAMS = 2
POOL_BM = 256
FFN_SUB = 256
FFN_BF = 256
DENSE_TILE = 1024
MOE_TILE = 2560
GATHER_BM = 256
DMA_UNROLL = 8


def _cparams(sem, vmem=None):
    return pltpu.CompilerParams(dimension_semantics=sem, vmem_limit_bytes=vmem)


def _split_bf16(x):
    hi = x.astype(BF16)
    lo = (x - hi.astype(F32)).astype(BF16)
    return hi, lo


def _rms_rows(x_ref, g_ref, h_ref, rows):
    def body(c, carry):
        r = pl.multiple_of(c * NORM_CHUNK, NORM_CHUNK)
        x = x_ref[pl.ds(r, NORM_CHUNK), :]
        ms = jnp.mean(x * x, axis=-1, keepdims=True)
        h_ref[pl.ds(r, NORM_CHUNK), :] = (x * lax.rsqrt(ms + NORM_EPS) * g_ref[...]).astype(h_ref.dtype)
        return carry
    lax.fori_loop(0, rows // NORM_CHUNK, body, 0)


def _in_proj_kernel(x_ref, g_ref, w_ref, qk_ref, v_ref, u_ref, h_ref, *, n_qk, n_v):
    j = pl.program_id(1)

    @pl.when(j == 0)
    def _():
        _rms_rows(x_ref, g_ref, h_ref, PROJ_BM)

    y = jnp.dot(h_ref[...], w_ref[...].astype(BF16), preferred_element_type=F32)

    @pl.when(j < n_qk)
    def _():
        qk_ref[...] = y

    @pl.when((j >= n_qk) & (j < n_qk + n_v))
    def _():
        v_ref[...] = y.astype(BF16)

    @pl.when(j >= n_qk + n_v)
    def _():
        u_ref[...] = y


def _in_proj(x, g, w_all, layer):
    m, d = x.shape
    bn = IN_PROJ_BN
    n_qk = 2 * ATTN_WIDTH // bn
    n_v = ATTN_WIDTH // bn
    n_u = POOL_WIDTH // bn
    assert w_all.shape[2] == (n_qk + n_v + n_u) * bn and m % PROJ_BM == 0
    return pl.pallas_call(
        functools.partial(_in_proj_kernel, n_qk=n_qk, n_v=n_v),
        grid=(m // PROJ_BM, n_qk + n_v + n_u),
        in_specs=[
            pl.BlockSpec((PROJ_BM, d), lambda i, j: (i, 0)),
            pl.BlockSpec((1, d), lambda i, j: (0, 0)),
            pl.BlockSpec((None, d, bn), lambda i, j: (layer, 0, j)),
        ],
        out_specs=[
            pl.BlockSpec((PROJ_BM, bn), lambda i, j: (i, jnp.minimum(j, n_qk - 1))),
            pl.BlockSpec((PROJ_BM, bn), lambda i, j: (i, jnp.clip(j - n_qk, 0, n_v - 1))),
            pl.BlockSpec((PROJ_BM, bn), lambda i, j: (i, jnp.clip(j - n_qk - n_v, 0, n_u - 1))),
        ],
        out_shape=[
            jax.ShapeDtypeStruct((m, 2 * ATTN_WIDTH), F32),
            jax.ShapeDtypeStruct((m, ATTN_WIDTH), BF16),
            jax.ShapeDtypeStruct((m, POOL_WIDTH), F32),
        ],
        scratch_shapes=[pltpu.VMEM((PROJ_BM, d), BF16)],
        compiler_params=_cparams(("arbitrary", "arbitrary"), VMEM_BUDGET),
        name="in_proj",
    )(x, g, w_all)


def _qk_rope_kernel(pos_ref, invf_ref, qk_ref, gain_ref, o_ref):
    bm = qk_ref.shape[0]
    ang = pos_ref[...].astype(F32) * invf_ref[...]
    cos = jnp.cos(ang)
    sin = jnp.sin(ang)
    lane = lax.broadcasted_iota(I32, (bm, LANES), 1)
    first_half = (lane % HEAD_DIM) < (HEAD_DIM // 2)
    sin_signed = jnp.where(first_half, -sin, sin)
    gr = lax.broadcasted_iota(I32, (LANES, LANES), 0) // HEAD_DIM
    gc = lax.broadcasted_iota(I32, (LANES, LANES), 1) // HEAD_DIM
    group_ones = jnp.where(gr == gc, 1.0, 0.0).astype(BF16)
    n_blocks = qk_ref.shape[1] // LANES
    for hb in range(n_blocks):
        cols = slice(hb * LANES, (hb + 1) * LANES)
        x = qk_ref[:, cols]
        hi, lo = _split_bf16(x * x)
        ssum = (jnp.dot(hi, group_ones, preferred_element_type=F32)
                + jnp.dot(lo, group_ones, preferred_element_type=F32))
        y = x * lax.rsqrt(ssum * (1.0 / HEAD_DIM) + NORM_EPS) * gain_ref[:, cols]
        swapped = jnp.where(first_half, pltpu.roll(y, LANES - HEAD_DIM // 2, 1),
                            pltpu.roll(y, HEAD_DIM // 2, 1))
        r = y * cos + swapped * sin_signed
        if hb < n_blocks // 2:
            r = r * (HEAD_DIM ** -0.5 * LOG2_E)
        o_ref[:, cols] = r.astype(BF16)


def _qk_rope(pos, invf, qk, gain):
    m, w = qk.shape
    return pl.pallas_call(
        _qk_rope_kernel,
        grid=(m // ROPE_BM,),
        in_specs=[
            pl.BlockSpec((ROPE_BM, 1), lambda i: (i, 0)),
            pl.BlockSpec((1, LANES), lambda i: (0, 0)),
            pl.BlockSpec((ROPE_BM, w), lambda i: (i, 0)),
            pl.BlockSpec((1, w), lambda i: (0, 0)),
        ],
        out_specs=pl.BlockSpec((ROPE_BM, w), lambda i: (i, 0)),
        out_shape=jax.ShapeDtypeStruct((m, w), BF16),
        compiler_params=_cparams(("arbitrary",)),
        name="qk_rope",
    )(pos, invf, qk, gain)


def _block_streams(nq):
    chains = [[(qi, kb) for kb in range(qi + 1)] for qi in range(nq - 1, -1, -1)]
    streams = tuple([] for _ in range(ATTN_STREAMS))
    for chain in chains:
        min(streams, key=len).extend(chain)
    order = []
    for i in range(max(len(s) for s in streams)):
        order.extend(s[i] for s in streams if i < len(s))
    return order


def _attn_kernel(linit_ref, q_ref, k_ref, v_ref, lq1_ref, lk1_ref, lq2_ref, lk2_ref, sg_ref, o_ref,
                 q2_ref, vt_ref, m_ref, l_ref, acc_ref):
    nq, _, bq = vt_ref.shape
    lane = lax.broadcasted_iota(I32, (bq, HEAD_WIDTH), 1)
    for c in range(nq):
        rows = slice(c * bq, (c + 1) * bq)
        vt_ref[c] = v_ref[rows, :].astype(F32).T.astype(BF16)
        q = q_ref[rows, :]
        zero = jnp.zeros_like(q)
        q2_ref[c, 0:bq, :] = jnp.where(lane < HEAD_DIM, q, zero)
        q2_ref[c, bq:2 * bq, :] = jnp.where(lane >= HEAD_DIM, q, zero)
    m_ref[...] = jnp.full(m_ref.shape, NEG_BIG, F32)
    l_ref[...] = jnp.zeros(l_ref.shape, F32)
    acc_ref[...] = jnp.zeros(acc_ref.shape, F32)

    for qi, kb in _block_streams(nq):
        k = k_ref[kb * bq:(kb + 1) * bq, :]
        s = lax.dot_general(k, q2_ref[qi], (((1,), (1,)), ((), ())), preferred_element_type=F32)
        if kb == qi:
            key = lax.broadcasted_iota(I32, s.shape, 0)
            qry = lax.broadcasted_iota(I32, s.shape, 1) & (bq - 1)
            s = jnp.where(key <= qry, s, NEG_BIG)
        m_old = m_ref[qi]
        m_new = jnp.maximum(m_old, jnp.max(s, axis=0, keepdims=True))
        alpha = jnp.exp2(m_old - m_new)
        p = jnp.exp2(s - m_new)
        l_ref[qi] = alpha * l_ref[qi] + jnp.sum(p, axis=0, keepdims=True)
        acc_ref[qi] = alpha * acc_ref[qi] + jnp.dot(vt_ref[kb], p.astype(BF16), preferred_element_type=F32)
        m_ref[qi] = m_new

    lambda_init = linit_ref[0]
    lam = (jnp.exp(jnp.sum(lq1_ref[...] * lk1_ref[...], axis=-1, keepdims=True))
           - jnp.exp(jnp.sum(lq2_ref[...] * lk2_ref[...], axis=-1, keepdims=True))
           + lambda_init)
    for qi in range(nq):
        acc = acc_ref[qi]
        l = l_ref[qi]
        o = acc[:, :bq] / l[:, :bq] - lam * (acc[:, bq:] / l[:, bq:])
        ms = jnp.mean(o * o, axis=0, keepdims=True)
        o = o * lax.rsqrt(ms + NORM_EPS) * sg_ref[...] * (1.0 - lambda_init)
        o_ref[qi * bq:(qi + 1) * bq, :] = o.T.astype(BF16)


def _attention(linit, qk, v, lq1, lk1, lq2, lk2, sg_col, batch, seq):
    m = qk.shape[0]
    nq = seq // ATTN_BQ
    vec = pl.BlockSpec((1, HEAD_DIM), lambda b, h: (0, 0))
    return pl.pallas_call(
        _attn_kernel,
        grid=(batch, N_HEADS),
        in_specs=[
            pl.BlockSpec(memory_space=pltpu.SMEM),
            pl.BlockSpec((seq, HEAD_WIDTH), lambda b, h: (b, h)),
            pl.BlockSpec((seq, HEAD_WIDTH), lambda b, h: (b, N_HEADS + h)),
            pl.BlockSpec((seq, HEAD_WIDTH), lambda b, h: (b, h)),
            vec, vec, vec, vec,
            pl.BlockSpec((HEAD_WIDTH, 1), lambda b, h: (0, 0)),
        ],
        out_specs=pl.BlockSpec((seq, HEAD_WIDTH), lambda b, h: (b, h)),
        out_shape=jax.ShapeDtypeStruct((m, ATTN_WIDTH), BF16),
        scratch_shapes=[
            pltpu.VMEM((nq, 2 * ATTN_BQ, HEAD_WIDTH), BF16),
            pltpu.VMEM((nq, HEAD_WIDTH, ATTN_BQ), BF16),
            pltpu.VMEM((nq, 1, 2 * ATTN_BQ), F32),
            pltpu.VMEM((nq, 1, 2 * ATTN_BQ), F32),
            pltpu.VMEM((nq, HEAD_WIDTH, 2 * ATTN_BQ), F32),
        ],
        compiler_params=_cparams(("arbitrary", "arbitrary")),
        name="diff_attn",
    )(linit, qk, qk, v, lq1, lk1, lq2, lk2, sg_col)


def _pool_kernel(uc_ref, up_ref, w_ref, b_ref, sc_ref, o_ref, *, chunks_per_seq):
    c = pl.program_id(0) % chunks_per_seq
    bm = uc_ref.shape[0]
    row = lax.broadcasted_iota(I32, (bm, bm), 0)
    col = lax.broadcasted_iota(I32, (bm, bm), 1)
    t = c * bm + lax.broadcasted_iota(I32, (bm, 1), 0)
    has_prev = c > 0
    for g, win in enumerate(POOL_WINDOWS):
        cols = slice(g * POOL_GROUP_DIM, (g + 1) * POOL_GROUP_DIM)
        cur = jnp.where((row >= col) & (row - col < win), 1.0, 0.0).astype(BF16)
        prv = jnp.where(col - row > bm - win, 1.0, 0.0).astype(BF16)
        u = uc_ref[:, cols]
        u_hi, u_lo = _split_bf16(u)
        p_hi, p_lo = _split_bf16(up_ref[:, cols])
        wsum = (jnp.dot(cur, u_hi, preferred_element_type=F32)
                + jnp.dot(cur, u_lo, preferred_element_type=F32))
        wprev = (jnp.dot(prv, p_hi, preferred_element_type=F32)
                 + jnp.dot(prv, p_lo, preferred_element_type=F32))
        wsum = wsum + jnp.where(has_prev, wprev, 0.0)
        cnt = jnp.minimum(t + 1, win).astype(F32)
        d = wsum / cnt - u
        y = jnp.dot(d.astype(BF16), w_ref[g].astype(BF16), preferred_element_type=F32) + b_ref[:, cols]
        o_ref[:, cols] = (y * sc_ref[:, cols]).astype(BF16)


def _pool(u, w_all, layer, b, sc, seq):
    m, width = u.shape
    cps = seq // POOL_BM
    return pl.pallas_call(
        functools.partial(_pool_kernel, chunks_per_seq=cps),
        grid=(m // POOL_BM,),
        in_specs=[
            pl.BlockSpec((POOL_BM, width), lambda i: (i, 0)),
            pl.BlockSpec((POOL_BM, width), lambda i: (jnp.maximum(i - 1, 0), 0)),
            pl.BlockSpec((None,) + w_all.shape[1:], lambda i: (layer, 0, 0, 0)),
            pl.BlockSpec((1, width), lambda i: (0, 0)),
            pl.BlockSpec((1, width), lambda i: (0, 0)),
        ],
        out_specs=pl.BlockSpec((POOL_BM, width), lambda i: (i, 0)),
        out_shape=jax.ShapeDtypeStruct((m, width), BF16),
        compiler_params=_cparams(("arbitrary",)),
        name="pool_mixer",
    )(u, u, w_all, b, sc)


def _out_proj_kernel(a_ref, p_ref, wa_ref, wp_ref, x_ref, o_ref):
    y = jnp.dot(a_ref[...], wa_ref[...].astype(BF16), preferred_element_type=F32)
    y = y + jnp.dot(p_ref[...], wp_ref[...].astype(BF16), preferred_element_type=F32)
    o_ref[...] = x_ref[...] + y


def _out_proj(a, p, w_all, layer, x):
    m, d = x.shape
    ka = a.shape[1]
    kp = p.shape[1]
    assert ka == kp
    return pl.pallas_call(
        _out_proj_kernel,
        grid=(m // PROJ_BM, d // PROJ_BN),
        in_specs=[
            pl.BlockSpec((PROJ_BM, ka), lambda i, j: (i, 0)),
            pl.BlockSpec((PROJ_BM, kp), lambda i, j: (i, 0)),
            pl.BlockSpec((None, ka, PROJ_BN), lambda i, j: (layer, 0, j)),
            pl.BlockSpec((None, kp, PROJ_BN), lambda i, j: (layer, 1, j)),
            pl.BlockSpec((PROJ_BM, PROJ_BN), lambda i, j: (i, j)),
        ],
        out_specs=pl.BlockSpec((PROJ_BM, PROJ_BN), lambda i, j: (i, j)),
        out_shape=jax.ShapeDtypeStruct((m, d), F32),
        compiler_params=_cparams(("arbitrary", "arbitrary"), VMEM_BUDGET),
        name="out_proj",
    )(a, p, w_all, w_all, x)


def _route_kernel(x_ref, g_ref, rw_ref, h_ref, idx_ref, gate_ref):
    x = x_ref[...]
    ms = jnp.mean(x * x, axis=-1, keepdims=True)
    h = x * lax.rsqrt(ms + NORM_EPS) * g_ref[...]
    h_ref[...] = h
    h_hi, h_lo = _split_bf16(h)
    w_hi, w_lo = _split_bf16(rw_ref[...])
    logits = (jnp.dot(h_hi, w_hi, preferred_element_type=F32)
              + jnp.dot(h_hi, w_lo, preferred_element_type=F32)
              + jnp.dot(h_lo, w_hi, preferred_element_type=F32))
    lane = lax.broadcasted_iota(I32, logits.shape, 1)
    logits = jnp.where(lane < N_EXPERTS, logits, NEG_BIG)
    v1 = jnp.max(logits, axis=-1, keepdims=True)
    i1 = jnp.min(jnp.where(logits == v1, lane, LANES), axis=-1, keepdims=True)
    rest = jnp.where(lane == i1, NEG_BIG, logits)
    v2 = jnp.max(rest, axis=-1, keepdims=True)
    i2 = jnp.min(jnp.where(rest == v2, lane, LANES), axis=-1, keepdims=True)
    e = jnp.exp(v2 - v1)
    g1 = 1.0 / (1.0 + e)
    g2 = e / (1.0 + e)
    idx_ref[...] = jnp.where(lane == 0, i1, jnp.where(lane == 1, i2, 0))
    gate_ref[...] = jnp.where(lane == 0, g1, jnp.where(lane == 1, g2, 0.0))


def _route(x, g, rw_padded):
    m, d = x.shape
    bm = 256
    return pl.pallas_call(
        _route_kernel,
        grid=(m // bm,),
        in_specs=[
            pl.BlockSpec((bm, d), lambda i: (i, 0)),
            pl.BlockSpec((1, d), lambda i: (0, 0)),
            pl.BlockSpec((d, LANES), lambda i: (0, 0)),
        ],
        out_specs=[
            pl.BlockSpec((bm, d), lambda i: (i, 0)),
            pl.BlockSpec((bm, LANES), lambda i: (i, 0)),
            pl.BlockSpec((bm, LANES), lambda i: (i, 0)),
        ],
        out_shape=[
            jax.ShapeDtypeStruct((m, d), F32),
            jax.ShapeDtypeStruct((m, LANES), I32),
            jax.ShapeDtypeStruct((m, LANES), F32),
        ],
        compiler_params=_cparams(("arbitrary",)),
        name="ffn_norm_route",
    )(x, g, rw_padded)


def _swiglu_kernel(eid_ref, nsub_ref, blk_ref, tot_ref, x_ref, w1_ref, w3_ref, w2_ref, *rest, has_res):
    if has_res:
        gain_ref, o_ref, w1s, w3s, w2s, h_ref = rest
    else:
        o_ref, w1s, w3s, w2s = rest
        h_ref = x_ref
    t = pl.program_id(0)
    j = pl.program_id(1)
    nsub = nsub_ref[t]

    @pl.when(j == 0)
    def _():
        if has_res:
            _rms_rows(x_ref, gain_ref, h_ref, x_ref.shape[0])

        def init(i, carry):
            rows = pl.ds(pl.multiple_of(i * FFN_SUB, FFN_SUB), FFN_SUB)
            if has_res:
                o_ref[rows, :] = x_ref[rows, :]
            else:
                o_ref[rows, :] = jnp.zeros((FFN_SUB, o_ref.shape[1]), F32)
            return carry
        lax.fori_loop(0, o_ref.shape[0] // FFN_SUB, init, 0)

    def sub_block(i):
        rows = pl.ds(pl.multiple_of(i * FFN_SUB, FFN_SUB), FFN_SUB)
        xs = h_ref[rows, :]
        a = jnp.dot(xs, w1s[...], preferred_element_type=F32)
        b = jnp.dot(xs, w3s[...], preferred_element_type=F32)
        hidden = (a * jax.nn.sigmoid(a) * b).astype(BF16)
        o_ref[rows, :] += jnp.dot(hidden, w2s[...], preferred_element_type=F32)

    def cast_weights():
        w1s[...] = w1_ref[0].astype(BF16)
        w3s[...] = w3_ref[0].astype(BF16)
        w2s[...] = w2_ref[0].astype(BF16)

    if has_res:
        cast_weights()
        for i in range(o_ref.shape[0] // FFN_SUB):
            sub_block(i)
    else:
        @pl.when(nsub > 0)
        def _():
            cast_weights()

            def quad(i, carry):
                for k in range(4):
                    sub_block(4 * i + k)
                return carry
            lax.fori_loop(0, nsub // 4, quad, 0)
            done = (nsub // 4) * 4

            @pl.when(nsub - done >= 2)
            def _():
                sub_block(done)
                sub_block(done + 1)

            @pl.when(nsub % 2 == 1)
            def _():
                sub_block(nsub - 1)


def _swiglu(meta, x, w1, w3, w2, gain, tile_rows):
    eid, nsub, blk, tot = meta
    n_tiles = eid.shape[0]
    d = x.shape[1]
    f = w1.shape[2]
    n_j = f // FFN_BF
    has_res = gain is not None

    def row_map(t, j, eid, nsub, blk, tot):
        return (blk[t], 0)

    def out_map(t, j, eid, nsub, blk, tot):
        return (t, 0)

    def hidden_block(t, j, tot):
        return jnp.where(t < tot[0], j, n_j - 1)

    def w13_map(t, j, eid, nsub, blk, tot):
        return (eid[t], 0, hidden_block(t, j, tot))

    def w2_map(t, j, eid, nsub, blk, tot):
        return (eid[t], hidden_block(t, j, tot), 0)

    once = pl.Buffered(1)
    in_specs = [
        pl.BlockSpec((tile_rows, d), row_map, pipeline_mode=once),
        pl.BlockSpec((1, d, FFN_BF), w13_map),
        pl.BlockSpec((1, d, FFN_BF), w13_map),
        pl.BlockSpec((1, FFN_BF, d), w2_map),
    ]
    args = [x, w1, w3, w2]
    scratch = [
        pltpu.VMEM((d, FFN_BF), BF16),
        pltpu.VMEM((d, FFN_BF), BF16),
        pltpu.VMEM((FFN_BF, d), BF16),
    ]
    if has_res:
        in_specs.append(pl.BlockSpec((1, d), lambda t, j, *_: (0, 0)))
        args.append(gain)
        scratch.append(pltpu.VMEM((tile_rows, d), BF16))
    return pl.pallas_call(
        functools.partial(_swiglu_kernel, has_res=has_res),
        grid_spec=pltpu.PrefetchScalarGridSpec(
            num_scalar_prefetch=4,
            grid=(n_tiles, n_j),
            in_specs=in_specs,
            out_specs=pl.BlockSpec((tile_rows, d), out_map, pipeline_mode=once),
            scratch_shapes=scratch,
        ),
        out_shape=jax.ShapeDtypeStruct((n_tiles * tile_rows, d), F32),
        compiler_params=_cparams(("arbitrary", "arbitrary"), VMEM_BUDGET),
        name="swiglu_res" if has_res else "swiglu_moe",
    )(eid, nsub, blk, tot, *args)


def _gather_kernel(nvalid_ref, tok_ref, tok_next_ref, h_hbm, o_ref, buf, sem):
    i = pl.program_id(0)
    nb = pl.num_programs(0)
    n = o_ref.shape[0]
    slot = i % 2
    nxt = jnp.minimum(i + 1, nb - 1)

    def row_copy(idx_ref, r, s):
        return pltpu.make_async_copy(h_hbm.at[pl.ds(idx_ref[0, 0, r], 1)], buf.at[s, pl.ds(r, 1)], sem.at[s])

    def start_all(idx_ref, s):
        def start(r, c):
            row_copy(idx_ref, r, s).start()
            return c
        lax.fori_loop(0, n, start, 0, unroll=DMA_UNROLL)

    @pl.when((i == 0) & (nvalid_ref[0] > 0))
    def _():
        start_all(tok_ref, 0)

    @pl.when((i + 1 < nb) & (nvalid_ref[nxt] > 0))
    def _():
        start_all(tok_next_ref, 1 - slot)

    valid = nvalid_ref[i] > 0

    @pl.when(valid)
    def _():
        def wait(r, c):
            row_copy(tok_ref, r, slot).wait()
            return c
        lax.fori_loop(0, n, wait, 0, unroll=DMA_UNROLL)
        o_ref[...] = buf[slot].astype(BF16)

    @pl.when(jnp.logical_not(valid))
    def _():
        o_ref[...] = jnp.zeros(o_ref.shape, BF16)


def _gather_rows(nvalid_blocks, tok_of_row, h):
    n_rows = tok_of_row.shape[0]
    d = h.shape[1]
    nb = n_rows // GATHER_BM
    tok3 = tok_of_row.reshape(nb, 1, GATHER_BM)
    idx_block = (1, 1, GATHER_BM)
    return pl.pallas_call(
        _gather_kernel,
        grid_spec=pltpu.PrefetchScalarGridSpec(
            num_scalar_prefetch=1,
            grid=(nb,),
            in_specs=[
                pl.BlockSpec(idx_block, lambda i, nv: (i, 0, 0), memory_space=pltpu.SMEM),
                pl.BlockSpec(idx_block, lambda i, nv: (jnp.minimum(i + 1, nb - 1), 0, 0), memory_space=pltpu.SMEM),
                pl.BlockSpec(memory_space=pl.ANY),
            ],
            out_specs=pl.BlockSpec((GATHER_BM, d), lambda i, nv: (i, 0)),
            scratch_shapes=[pltpu.VMEM((2, GATHER_BM, d), F32), pltpu.SemaphoreType.DMA((2,))],
        ),
        out_shape=jax.ShapeDtypeStruct((n_rows, d), BF16),
        compiler_params=_cparams(("arbitrary",)),
        name="moe_gather",
    )(nvalid_blocks, tok3, tok3, h)


def _combine_kernel(pos_ref, pos_next_ref, x_ref, gate_ref, y_hbm, o_ref, buf, sem):
    i = pl.program_id(0)
    nb = pl.num_programs(0)
    n = o_ref.shape[0]
    slot = i % 2

    def row_copy(idx_ref, r, k, s):
        return pltpu.make_async_copy(y_hbm.at[pl.ds(idx_ref[0, k, r], 1)], buf.at[s, k, pl.ds(r, 1)], sem.at[s])

    def start_all(idx_ref, s):
        def start(r, c):
            row_copy(idx_ref, r, 0, s).start()
            row_copy(idx_ref, r, 1, s).start()
            return c
        lax.fori_loop(0, n, start, 0, unroll=DMA_UNROLL)

    @pl.when(i == 0)
    def _():
        start_all(pos_ref, 0)

    @pl.when(i + 1 < nb)
    def _():
        start_all(pos_next_ref, 1 - slot)

    def wait(r, c):
        row_copy(pos_ref, r, 0, slot).wait()
        row_copy(pos_ref, r, 1, slot).wait()
        return c
    lax.fori_loop(0, n, wait, 0, unroll=DMA_UNROLL)
    g = gate_ref[...]
    o_ref[...] = x_ref[...] + g[:, 0:1] * buf[slot, 0] + g[:, 1:2] * buf[slot, 1]


def _combine(pos, x, gate, y):
    m, d = x.shape
    nb = m // GATHER_BM
    pos3 = pos.reshape(nb, GATHER_BM, TOP_K).transpose(0, 2, 1)
    idx_block = (1, TOP_K, GATHER_BM)
    return pl.pallas_call(
        _combine_kernel,
        grid=(nb,),
        in_specs=[
            pl.BlockSpec(idx_block, lambda i: (i, 0, 0), memory_space=pltpu.SMEM),
            pl.BlockSpec(idx_block, lambda i: (jnp.minimum(i + 1, nb - 1), 0, 0), memory_space=pltpu.SMEM),
            pl.BlockSpec((GATHER_BM, d), lambda i: (i, 0)),
            pl.BlockSpec((GATHER_BM, LANES), lambda i: (i, 0)),
            pl.BlockSpec(memory_space=pl.ANY),
        ],
        out_specs=pl.BlockSpec((GATHER_BM, d), lambda i: (i, 0)),
        out_shape=jax.ShapeDtypeStruct((m, d), F32),
        scratch_shapes=[pltpu.VMEM((2, TOP_K, GATHER_BM, d), F32), pltpu.SemaphoreType.DMA((2,))],
        compiler_params=_cparams(("arbitrary",)),
        name="moe_combine",
    )(pos3, pos3, x, gate, y)


def _dispatch_plan(idx, n_tokens):
    n_assign = n_tokens * TOP_K
    n_tiles = n_assign // MOE_TILE + N_EXPERTS
    e_flat = idx.reshape(n_assign)
    onehot = (e_flat[:, None] == jnp.arange(N_EXPERTS, dtype=I32)[None, :]).astype(I32)
    counts = jnp.sum(onehot, axis=0)
    rank = jnp.take_along_axis(jnp.cumsum(onehot, axis=0) - onehot, e_flat[:, None], axis=1)[:, 0]
    tiles_e = (counts + MOE_TILE - 1) // MOE_TILE
    tile_end = jnp.cumsum(tiles_e)
    tile_start = tile_end - tiles_e
    total = tile_end[-1]
    pos = tile_start[e_flat] * MOE_TILE + rank
    t = jnp.arange(n_tiles, dtype=I32)
    t_eff = jnp.minimum(t, total - 1)
    eid = jnp.minimum(jnp.sum((t_eff[:, None] >= tile_end[None, :]).astype(I32), axis=1), N_EXPERTS - 1)
    rows_valid = jnp.clip(counts[eid] - (t_eff - tile_start[eid]) * MOE_TILE, 0, MOE_TILE)
    rows_valid = jnp.where(t < total, rows_valid, 0)
    nsub = (rows_valid + FFN_SUB - 1) // FFN_SUB
    sub_per_tile = MOE_TILE // GATHER_BM
    sub_id = jnp.arange(n_tiles * sub_per_tile, dtype=I32)
    nvalid_blocks = ((sub_id % sub_per_tile) < nsub[sub_id // sub_per_tile]).astype(I32)
    tok_of_row = jnp.zeros((n_tiles * MOE_TILE,), I32).at[pos].set(jnp.arange(n_assign, dtype=I32) // TOP_K)
    meta = (eid, nsub.astype(I32), t_eff, jnp.reshape(total, (1,)).astype(I32))
    return meta, nvalid_blocks, tok_of_row, pos.reshape(n_tokens, TOP_K)


def _moe(x, g, rw, w1_all, w3_all, w2_all, layer_idx):
    m, d = x.shape
    w1 = w1_all.reshape((-1,) + w1_all.shape[2:])
    w3 = w3_all.reshape((-1,) + w3_all.shape[2:])
    w2 = w2_all.reshape((-1,) + w2_all.shape[2:])
    rw_padded = jnp.zeros((d, LANES), F32).at[:, :N_EXPERTS].set(rw)
    h, idx, gate = _route(x, g, rw_padded)
    meta, nvalid_blocks, tok_of_row, pos = _dispatch_plan(idx[:, :TOP_K], m)
    meta = (meta[0] + layer_idx * N_EXPERTS,) + meta[1:]
    xs = _gather_rows(nvalid_blocks, tok_of_row, h)
    y = _swiglu(meta, xs, w1, w3, w2, None, MOE_TILE)
    return _combine(pos, x, gate, y)


def _dense(x, g, w1_all, w3_all, w2_all, layer_idx):
    m, d = x.shape
    n_tiles = m // DENSE_TILE
    meta = (jnp.full((n_tiles,), layer_idx, I32), jnp.full((n_tiles,), DENSE_TILE // FFN_SUB, I32),
            jnp.arange(n_tiles, dtype=I32), jnp.full((1,), n_tiles, I32))
    return _swiglu(meta, x, w1_all, w3_all, w2_all, g, DENSE_TILE)


def kernel(x, positions, attn_norm, w_in, q_norm, k_norm, lambda_q1, lambda_k1, lambda_q2, lambda_k2, subln, pool_w, pool_b, pool_scale, w_out, ffn_norm, dense_w1, dense_w3, dense_w2, router_w, moe_w1, moe_w3, moe_w2):
    batch, seq, d = x.shape
    depth = w_in.shape[0]
    m = batch * seq
    xf = x.reshape(m, d)
    pos = positions.reshape(m, 1)
    inv_freq = 1.0 / (ROPE_THETA ** (jnp.arange(0, HEAD_DIM, 2, dtype=F32) / HEAD_DIM))
    invf = jnp.tile(inv_freq, LANES // (HEAD_DIM // 2))[None, :]
    for l in range(depth):
        qk, v, u = _in_proj(xf, attn_norm[l][None, :], w_in, l)
        gain = jnp.concatenate([jnp.tile(q_norm[l], ATTN_WIDTH // HEAD_DIM),
                                jnp.tile(k_norm[l], ATTN_WIDTH // HEAD_DIM)])[None, :]
        qk = _qk_rope(pos, invf, qk, gain)
        lambda_init = 0.8 - 0.6 * math.exp(-0.3 * l)
        a = _attention(jnp.full((1,), lambda_init, F32), qk, v, lambda_q1[l][None, :], lambda_k1[l][None, :],
                       lambda_q2[l][None, :], lambda_k2[l][None, :], subln[l][:, None], batch, seq)
        p = _pool(u, pool_w, l, pool_b[l].reshape(1, POOL_WIDTH), pool_scale[l][None, :], seq)
        xf = _out_proj(a, p, w_out, l, xf)
        i = l // 2
        if l % 2 == 0:
            xf = _dense(xf, ffn_norm[l][None, :], dense_w1, dense_w3, dense_w2, i)
        else:
            xf = _moe(xf, ffn_norm[l][None, :], router_w[i], moe_w1, moe_w3, moe_w2, i)
    return xf.reshape(batch, seq, d)
```

```python
import functools
import math

import jax
import jax.numpy as jnp
from jax import lax
from jax.experimental import pallas as pl
from jax.experimental.pallas import tpu as pltpu

F32 = jnp.float32
BF16 = jnp.bfloat16
I32 = jnp.int32

D_MODEL = 2048
ATTN_WIDTH = 1024
POOL_WIDTH = 1024
HEAD_DIM = 64
N_HEADS = 8
HEAD_WIDTH = 2 * HEAD_DIM
POOL_WINDOWS = (2, 4, 8, 16)
POOL_GROUP_DIM = 256
ROPE_THETA = 10000.0
NORM_EPS = 1e-6
N_EXPERTS = 8
TOP_K = 2
NEG_BIG = -1e30
LOG2_E = math.log2(math.e)

LANES = 128
VMEM_BUDGET = 56 * 1024 * 1024

PROJ_BM = 1024
PROJ_BN = 1024
IN_PROJ_BN = 512
NORM_CHUNK = 128
ROPE_BM = 256
ATTN_BQ = 512
ATTN_STREAMS = 2
POOL_BM = 256
FFN_SUB = 256
FFN_MM = 512
FFN_TRIP = 4
FFN_BF = 256
DENSE_TILE = 2048
MOE_TILE = 2560
GATHER_BM = 256
DMA_UNROLL = 8


def _cparams(sem, vmem=None):
    return pltpu.CompilerParams(dimension_semantics=sem, vmem_limit_bytes=vmem)


def _split_bf16(x):
    hi = x.astype(BF16)
    lo = (x - hi.astype(F32)).astype(BF16)
    return hi, lo


def _rms_rows(x_ref, g_ref, h_ref, rows):
    def body(c, carry):
        r = pl.multiple_of(c * NORM_CHUNK, NORM_CHUNK)
        x = x_ref[pl.ds(r, NORM_CHUNK), :]
        ms = jnp.mean(x * x, axis=-1, keepdims=True)
        h_ref[pl.ds(r, NORM_CHUNK), :] = (x * lax.rsqrt(ms + NORM_EPS) * g_ref[...]).astype(h_ref.dtype)
        return carry
    lax.fori_loop(0, rows // NORM_CHUNK, body, 0)


def _in_proj_kernel(x_ref, g_ref, w_ref, qk_ref, v_ref, u_ref, h_ref, *, n_qk, n_v):
    j = pl.program_id(1)

    @pl.when(j == 0)
    def _():
        _rms_rows(x_ref, g_ref, h_ref, PROJ_BM)

    y = jnp.dot(h_ref[...], w_ref[...].astype(BF16), preferred_element_type=F32)

    @pl.when(j < n_qk)
    def _():
        qk_ref[...] = y

    @pl.when((j >= n_qk) & (j < n_qk + n_v))
    def _():
        v_ref[...] = y.astype(BF16)

    @pl.when(j >= n_qk + n_v)
    def _():
        u_ref[...] = y


def _in_proj(x, g, w_all, layer):
    m, d = x.shape
    bn = IN_PROJ_BN
    n_qk = 2 * ATTN_WIDTH // bn
    n_v = ATTN_WIDTH // bn
    n_u = POOL_WIDTH // bn
    assert w_all.shape[2] == (n_qk + n_v + n_u) * bn and m % PROJ_BM == 0
    return pl.pallas_call(
        functools.partial(_in_proj_kernel, n_qk=n_qk, n_v=n_v),
        grid=(m // PROJ_BM, n_qk + n_v + n_u),
        in_specs=[
            pl.BlockSpec((PROJ_BM, d), lambda i, j: (i, 0)),
            pl.BlockSpec((1, d), lambda i, j: (0, 0)),
            pl.BlockSpec((None, d, bn), lambda i, j: (layer, 0, j)),
        ],
        out_specs=[
            pl.BlockSpec((PROJ_BM, bn), lambda i, j: (i, jnp.minimum(j, n_qk - 1))),
            pl.BlockSpec((PROJ_BM, bn), lambda i, j: (i, jnp.clip(j - n_qk, 0, n_v - 1))),
            pl.BlockSpec((PROJ_BM, bn), lambda i, j: (i, jnp.clip(j - n_qk - n_v, 0, n_u - 1))),
        ],
        out_shape=[
            jax.ShapeDtypeStruct((m, 2 * ATTN_WIDTH), F32),
            jax.ShapeDtypeStruct((m, ATTN_WIDTH), BF16),
            jax.ShapeDtypeStruct((m, POOL_WIDTH), F32),
        ],
        scratch_shapes=[pltpu.VMEM((PROJ_BM, d), BF16)],
        compiler_params=_cparams(("arbitrary", "arbitrary"), VMEM_BUDGET),
        name="in_proj",
    )(x, g, w_all)


def _qk_rope_kernel(pos_ref, invf_ref, qk_ref, gain_ref, o_ref):
    bm = qk_ref.shape[0]
    ang = pos_ref[...].astype(F32) * invf_ref[...]
    cos = jnp.cos(ang)
    sin = jnp.sin(ang)
    lane = lax.broadcasted_iota(I32, (bm, LANES), 1)
    first_half = (lane % HEAD_DIM) < (HEAD_DIM // 2)
    sin_signed = jnp.where(first_half, -sin, sin)
    gr = lax.broadcasted_iota(I32, (LANES, LANES), 0) // HEAD_DIM
    gc = lax.broadcasted_iota(I32, (LANES, LANES), 1) // HEAD_DIM
    group_ones = jnp.where(gr == gc, 1.0, 0.0).astype(BF16)
    n_blocks = qk_ref.shape[1] // LANES
    for hb in range(n_blocks):
        cols = slice(hb * LANES, (hb + 1) * LANES)
        x = qk_ref[:, cols]
        hi, lo = _split_bf16(x * x)
        ssum = (jnp.dot(hi, group_ones, preferred_element_type=F32)
                + jnp.dot(lo, group_ones, preferred_element_type=F32))
        y = x * lax.rsqrt(ssum * (1.0 / HEAD_DIM) + NORM_EPS) * gain_ref[:, cols]
        swapped = jnp.where(first_half, pltpu.roll(y, LANES - HEAD_DIM // 2, 1),
                            pltpu.roll(y, HEAD_DIM // 2, 1))
        r = y * cos + swapped * sin_signed
        if hb < n_blocks // 2:
            r = r * (HEAD_DIM ** -0.5 * LOG2_E)
        o_ref[:, cols] = r.astype(BF16)


def _qk_rope(pos, invf, qk, gain):
    m, w = qk.shape
    return pl.pallas_call(
        _qk_rope_kernel,
        grid=(m // ROPE_BM,),
        in_specs=[
            pl.BlockSpec((ROPE_BM, 1), lambda i: (i, 0)),
            pl.BlockSpec((1, LANES), lambda i: (0, 0)),
            pl.BlockSpec((ROPE_BM, w), lambda i: (i, 0)),
            pl.BlockSpec((1, w), lambda i: (0, 0)),
        ],
        out_specs=pl.BlockSpec((ROPE_BM, w), lambda i: (i, 0)),
        out_shape=jax.ShapeDtypeStruct((m, w), BF16),
        compiler_params=_cparams(("arbitrary",)),
        name="qk_rope",
    )(pos, invf, qk, gain)


def _block_streams(nq):
    chains = [[(qi, kb) for kb in range(qi + 1)] for qi in range(nq - 1, -1, -1)]
    streams = tuple([] for _ in range(ATTN_STREAMS))
    for chain in chains:
        min(streams, key=len).extend(chain)
    order = []
    for i in range(max(len(s) for s in streams)):
        order.extend(s[i] for s in streams if i < len(s))
    return order


def _attn_kernel(linit_ref, q_ref, k_ref, v_ref, lq1_ref, lk1_ref, lq2_ref, lk2_ref, sg_ref, o_ref,
                 q2_ref, vt_ref, m_ref, l_ref, acc_ref):
    nq, _, bq = vt_ref.shape
    lane = lax.broadcasted_iota(I32, (bq, HEAD_WIDTH), 1)
    for c in range(nq):
        rows = slice(c * bq, (c + 1) * bq)
        vt_ref[c] = v_ref[rows, :].astype(F32).T.astype(BF16)
        q = q_ref[rows, :]
        zero = jnp.zeros_like(q)
        q2_ref[c, 0:bq, :] = jnp.where(lane < HEAD_DIM, q, zero)
        q2_ref[c, bq:2 * bq, :] = jnp.where(lane >= HEAD_DIM, q, zero)
    m_ref[...] = jnp.full(m_ref.shape, NEG_BIG, F32)
    l_ref[...] = jnp.zeros(l_ref.shape, F32)
    acc_ref[...] = jnp.zeros(acc_ref.shape, F32)

    for qi, kb in _block_streams(nq):
        k = k_ref[kb * bq:(kb + 1) * bq, :]
        s = lax.dot_general(k, q2_ref[qi], (((1,), (1,)), ((), ())), preferred_element_type=F32)
        if kb == qi:
            key = lax.broadcasted_iota(I32, s.shape, 0)
            qry = lax.broadcasted_iota(I32, s.shape, 1) & (bq - 1)
            s = jnp.where(key <= qry, s, NEG_BIG)
        m_old = m_ref[qi]
        m_new = jnp.maximum(m_old, jnp.max(s, axis=0, keepdims=True))
        alpha = jnp.exp2(m_old - m_new)
        p = jnp.exp2(s - m_new)
        l_ref[qi] = alpha * l_ref[qi] + jnp.sum(p, axis=0, keepdims=True)
        acc_ref[qi] = alpha * acc_ref[qi] + jnp.dot(vt_ref[kb], p.astype(BF16), preferred_element_type=F32)
        m_ref[qi] = m_new

    lambda_init = linit_ref[0]
    lam = (jnp.exp(jnp.sum(lq1_ref[...] * lk1_ref[...], axis=-1, keepdims=True))
           - jnp.exp(jnp.sum(lq2_ref[...] * lk2_ref[...], axis=-1, keepdims=True))
           + lambda_init)
    for qi in range(nq):
        acc = acc_ref[qi]
        l = l_ref[qi]
        o = acc[:, :bq] / l[:, :bq] - lam * (acc[:, bq:] / l[:, bq:])
        ms = jnp.mean(o * o, axis=0, keepdims=True)
        o = o * lax.rsqrt(ms + NORM_EPS) * sg_ref[...] * (1.0 - lambda_init)
        o_ref[qi * bq:(qi + 1) * bq, :] = o.T.astype(BF16)


def _attention(linit, qk, v, lq1, lk1, lq2, lk2, sg_col, batch, seq):
    m = qk.shape[0]
    nq = seq // ATTN_BQ
    vec = pl.BlockSpec((1, HEAD_DIM), lambda b, h: (0, 0))
    return pl.pallas_call(
        _attn_kernel,
        grid=(batch, N_HEADS),
        in_specs=[
            pl.BlockSpec(memory_space=pltpu.SMEM),
            pl.BlockSpec((seq, HEAD_WIDTH), lambda b, h: (b, h)),
            pl.BlockSpec((seq, HEAD_WIDTH), lambda b, h: (b, N_HEADS + h)),
            pl.BlockSpec((seq, HEAD_WIDTH), lambda b, h: (b, h)),
            vec, vec, vec, vec,
            pl.BlockSpec((HEAD_WIDTH, 1), lambda b, h: (0, 0)),
        ],
        out_specs=pl.BlockSpec((seq, HEAD_WIDTH), lambda b, h: (b, h)),
        out_shape=jax.ShapeDtypeStruct((m, ATTN_WIDTH), BF16),
        scratch_shapes=[
            pltpu.VMEM((nq, 2 * ATTN_BQ, HEAD_WIDTH), BF16),
            pltpu.VMEM((nq, HEAD_WIDTH, ATTN_BQ), BF16),
            pltpu.VMEM((nq, 1, 2 * ATTN_BQ), F32),
            pltpu.VMEM((nq, 1, 2 * ATTN_BQ), F32),
            pltpu.VMEM((nq, HEAD_WIDTH, 2 * ATTN_BQ), F32),
        ],
        compiler_params=_cparams(("arbitrary", "arbitrary")),
        name="diff_attn",
    )(linit, qk, qk, v, lq1, lk1, lq2, lk2, sg_col)


def _pool_kernel(uc_ref, up_ref, w_ref, b_ref, sc_ref, o_ref, *, chunks_per_seq):
    c = pl.program_id(0) % chunks_per_seq
    bm = uc_ref.shape[0]
    row = lax.broadcasted_iota(I32, (bm, bm), 0)
    col = lax.broadcasted_iota(I32, (bm, bm), 1)
    t = c * bm + lax.broadcasted_iota(I32, (bm, 1), 0)
    has_prev = c > 0
    for g, win in enumerate(POOL_WINDOWS):
        cols = slice(g * POOL_GROUP_DIM, (g + 1) * POOL_GROUP_DIM)
        cur = jnp.where((row >= col) & (row - col < win), 1.0, 0.0).astype(BF16)
        prv = jnp.where(col - row > bm - win, 1.0, 0.0).astype(BF16)
        u = uc_ref[:, cols]
        u_hi, u_lo = _split_bf16(u)
        p_hi, p_lo = _split_bf16(up_ref[:, cols])
        wsum = (jnp.dot(cur, u_hi, preferred_element_type=F32)
                + jnp.dot(cur, u_lo, preferred_element_type=F32))
        wprev = (jnp.dot(prv, p_hi, preferred_element_type=F32)
                 + jnp.dot(prv, p_lo, preferred_element_type=F32))
        wsum = wsum + jnp.where(has_prev, wprev, 0.0)
        cnt = jnp.minimum(t + 1, win).astype(F32)
        d = wsum / cnt - u
        y = jnp.dot(d.astype(BF16), w_ref[g].astype(BF16), preferred_element_type=F32) + b_ref[:, cols]
        o_ref[:, cols] = (y * sc_ref[:, cols]).astype(BF16)


def _pool(u, w_all, layer, b, sc, seq):
    m, width = u.shape
    cps = seq // POOL_BM
    return pl.pallas_call(
        functools.partial(_pool_kernel, chunks_per_seq=cps),
        grid=(m // POOL_BM,),
        in_specs=[
            pl.BlockSpec((POOL_BM, width), lambda i: (i, 0)),
            pl.BlockSpec((POOL_BM, width), lambda i: (jnp.maximum(i - 1, 0), 0)),
            pl.BlockSpec((None,) + w_all.shape[1:], lambda i: (layer, 0, 0, 0)),
            pl.BlockSpec((1, width), lambda i: (0, 0)),
            pl.BlockSpec((1, width), lambda i: (0, 0)),
        ],
        out_specs=pl.BlockSpec((POOL_BM, width), lambda i: (i, 0)),
        out_shape=jax.ShapeDtypeStruct((m, width), BF16),
        compiler_params=_cparams(("arbitrary",)),
        name="pool_mixer",
    )(u, u, w_all, b, sc)


def _out_proj_kernel(a_ref, p_ref, wa_ref, wp_ref, x_ref, o_ref):
    y = jnp.dot(a_ref[...], wa_ref[...].astype(BF16), preferred_element_type=F32)
    y = y + jnp.dot(p_ref[...], wp_ref[...].astype(BF16), preferred_element_type=F32)
    o_ref[...] = x_ref[...] + y


def _out_proj(a, p, w_all, layer, x):
    m, d = x.shape
    ka = a.shape[1]
    kp = p.shape[1]
    assert ka == kp
    return pl.pallas_call(
        _out_proj_kernel,
        grid=(m // PROJ_BM, d // PROJ_BN),
        in_specs=[
            pl.BlockSpec((PROJ_BM, ka), lambda i, j: (i, 0)),
            pl.BlockSpec((PROJ_BM, kp), lambda i, j: (i, 0)),
            pl.BlockSpec((None, ka, PROJ_BN), lambda i, j: (layer, 0, j)),
            pl.BlockSpec((None, kp, PROJ_BN), lambda i, j: (layer, 1, j)),
            pl.BlockSpec((PROJ_BM, PROJ_BN), lambda i, j: (i, j)),
        ],
        out_specs=pl.BlockSpec((PROJ_BM, PROJ_BN), lambda i, j: (i, j)),
        out_shape=jax.ShapeDtypeStruct((m, d), F32),
        compiler_params=_cparams(("arbitrary", "arbitrary"), VMEM_BUDGET),
        name="out_proj",
    )(a, p, w_all, w_all, x)


def _route_kernel(x_ref, g_ref, rw_ref, h_ref, idx_ref, gate_ref):
    x = x_ref[...]
    ms = jnp.mean(x * x, axis=-1, keepdims=True)
    h = x * lax.rsqrt(ms + NORM_EPS) * g_ref[...]
    h_ref[...] = h
    h_hi, h_lo = _split_bf16(h)
    w_hi, w_lo = _split_bf16(rw_ref[...])
    logits = (jnp.dot(h_hi, w_hi, preferred_element_type=F32)
              + jnp.dot(h_hi, w_lo, preferred_element_type=F32)
              + jnp.dot(h_lo, w_hi, preferred_element_type=F32))
    lane = lax.broadcasted_iota(I32, logits.shape, 1)
    logits = jnp.where(lane < N_EXPERTS, logits, NEG_BIG)
    v1 = jnp.max(logits, axis=-1, keepdims=True)
    i1 = jnp.min(jnp.where(logits == v1, lane, LANES), axis=-1, keepdims=True)
    rest = jnp.where(lane == i1, NEG_BIG, logits)
    v2 = jnp.max(rest, axis=-1, keepdims=True)
    i2 = jnp.min(jnp.where(rest == v2, lane, LANES), axis=-1, keepdims=True)
    e = jnp.exp(v2 - v1)
    g1 = 1.0 / (1.0 + e)
    g2 = e / (1.0 + e)
    idx_ref[...] = jnp.where(lane == 0, i1, jnp.where(lane == 1, i2, 0))
    gate_ref[...] = jnp.where(lane == 0, g1, jnp.where(lane == 1, g2, 0.0))


def _route(x, g, rw_padded):
    m, d = x.shape
    bm = 256
    return pl.pallas_call(
        _route_kernel,
        grid=(m // bm,),
        in_specs=[
            pl.BlockSpec((bm, d), lambda i: (i, 0)),
            pl.BlockSpec((1, d), lambda i: (0, 0)),
            pl.BlockSpec((d, LANES), lambda i: (0, 0)),
        ],
        out_specs=[
            pl.BlockSpec((bm, d), lambda i: (i, 0)),
            pl.BlockSpec((bm, LANES), lambda i: (i, 0)),
            pl.BlockSpec((bm, LANES), lambda i: (i, 0)),
        ],
        out_shape=[
            jax.ShapeDtypeStruct((m, d), F32),
            jax.ShapeDtypeStruct((m, LANES), I32),
            jax.ShapeDtypeStruct((m, LANES), F32),
        ],
        compiler_params=_cparams(("arbitrary",)),
        name="ffn_norm_route",
    )(x, g, rw_padded)


def _swiglu_kernel(eid_ref, nsub_ref, blk_ref, tot_ref, x_ref, w1_ref, w3_ref, w2_ref, *rest, has_res):
    if has_res:
        gain_ref, o_ref, w1s, w3s, w2s, h_ref, sem = rest
    else:
        o_ref, w1s, w3s, w2s = rest
        h_ref = x_ref
    t = pl.program_id(0)
    j = pl.program_id(1)
    nsub = nsub_ref[t]
    tile_rows = o_ref.shape[0]

    @pl.when(j == 0)
    def _():
        if has_res:
            seed = pltpu.make_async_copy(
                x_ref.at[pl.ds(pl.multiple_of(t * tile_rows, tile_rows), tile_rows)], o_ref, sem)
            seed.start()
            seed.wait()
            _rms_rows(o_ref, gain_ref, h_ref, tile_rows)
        else:
            def init(i, carry):
                rows = pl.ds(pl.multiple_of(i * FFN_SUB, FFN_SUB), FFN_SUB)
                o_ref[rows, :] = jnp.zeros((FFN_SUB, o_ref.shape[1]), F32)
                return carry
            lax.fori_loop(0, tile_rows // FFN_SUB, init, 0)

    def row_block(start, n_rows):
        rows = pl.ds(pl.multiple_of(start, FFN_SUB), n_rows)
        xs = h_ref[rows, :]
        a = jnp.dot(xs, w1s[...], preferred_element_type=F32)
        b = jnp.dot(xs, w3s[...], preferred_element_type=F32)
        hidden = (a * jax.nn.sigmoid(a) * b).astype(BF16)
        o_ref[rows, :] += jnp.dot(hidden, w2s[...], preferred_element_type=F32)

    def cast_weights():
        w1s[...] = w1_ref[0].astype(BF16)
        w3s[...] = w3_ref[0].astype(BF16)
        w2s[...] = w2_ref[0].astype(BF16)

    if has_res:
        cast_weights()
        for i in range(o_ref.shape[0] // FFN_MM):
            row_block(i * FFN_MM, FFN_MM)
    else:
        @pl.when(nsub > 0)
        def _():
            cast_weights()
            n_mm = nsub // (FFN_MM // FFN_SUB)
            n_trips = n_mm // FFN_TRIP

            def trip(i, carry):
                for k in range(FFN_TRIP):
                    row_block((i * FFN_TRIP + k) * FFN_MM, FFN_MM)
                return carry
            lax.fori_loop(0, n_trips, trip, 0)

            def single(i, carry):
                row_block(i * FFN_MM, FFN_MM)
                return carry
            lax.fori_loop(n_trips * FFN_TRIP, n_mm, single, 0)

            @pl.when(nsub % (FFN_MM // FFN_SUB) == 1)
            def _():
                row_block((nsub - 1) * FFN_SUB, FFN_SUB)


def _swiglu(meta, x, w1, w3, w2, gain, tile_rows):
    eid, nsub, blk, tot = meta
    n_tiles = eid.shape[0]
    d = x.shape[1]
    f = w1.shape[2]
    n_j = f // FFN_BF
    has_res = gain is not None

    def row_map(t, j, eid, nsub, blk, tot):
        return (blk[t], 0)

    def out_map(t, j, eid, nsub, blk, tot):
        return (t, 0)

    def hidden_block(t, j, tot):
        return jnp.where(t < tot[0], j, n_j - 1)

    def w13_map(t, j, eid, nsub, blk, tot):
        return (eid[t], 0, hidden_block(t, j, tot))

    def w2_map(t, j, eid, nsub, blk, tot):
        return (eid[t], hidden_block(t, j, tot), 0)

    once = pl.Buffered(1)
    in_specs = [
        pl.BlockSpec(memory_space=pl.ANY) if has_res else pl.BlockSpec((tile_rows, d), row_map, pipeline_mode=once),
        pl.BlockSpec((1, d, FFN_BF), w13_map),
        pl.BlockSpec((1, d, FFN_BF), w13_map),
        pl.BlockSpec((1, FFN_BF, d), w2_map),
    ]
    args = [x, w1, w3, w2]
    scratch = [
        pltpu.VMEM((d, FFN_BF), BF16),
        pltpu.VMEM((d, FFN_BF), BF16),
        pltpu.VMEM((FFN_BF, d), BF16),
    ]
    if has_res:
        in_specs.append(pl.BlockSpec((1, d), lambda t, j, *_: (0, 0)))
        args.append(gain)
        scratch.append(pltpu.VMEM((tile_rows, d), BF16))
        scratch.append(pltpu.SemaphoreType.DMA(()))
    return pl.pallas_call(
        functools.partial(_swiglu_kernel, has_res=has_res),
        grid_spec=pltpu.PrefetchScalarGridSpec(
            num_scalar_prefetch=4,
            grid=(n_tiles, n_j),
            in_specs=in_specs,
            out_specs=pl.BlockSpec((tile_rows, d), out_map, pipeline_mode=once),
            scratch_shapes=scratch,
        ),
        out_shape=jax.ShapeDtypeStruct((n_tiles * tile_rows, d), F32),
        compiler_params=_cparams(("arbitrary", "arbitrary"), VMEM_BUDGET),
        name="swiglu_res" if has_res else "swiglu_moe",
    )(eid, nsub, blk, tot, *args)


def _gather_kernel(nvalid_ref, tok_ref, tok_next_ref, h_hbm, o_ref, buf, sem):
    i = pl.program_id(0)
    nb = pl.num_programs(0)
    n = o_ref.shape[0]
    slot = i % 2
    nxt = jnp.minimum(i + 1, nb - 1)

    def row_copy(idx_ref, r, s):
        return pltpu.make_async_copy(h_hbm.at[pl.ds(idx_ref[0, 0, r], 1)], buf.at[s, pl.ds(r, 1)], sem.at[s])

    def start_all(idx_ref, s):
        def start(r, c):
            row_copy(idx_ref, r, s).start()
            return c
        lax.fori_loop(0, n, start, 0, unroll=DMA_UNROLL)

    @pl.when((i == 0) & (nvalid_ref[0] > 0))
    def _():
        start_all(tok_ref, 0)

    @pl.when((i + 1 < nb) & (nvalid_ref[nxt] > 0))
    def _():
        start_all(tok_next_ref, 1 - slot)

    valid = nvalid_ref[i] > 0

    @pl.when(valid)
    def _():
        def wait(r, c):
            row_copy(tok_ref, r, slot).wait()
            return c
        lax.fori_loop(0, n, wait, 0, unroll=DMA_UNROLL)
        o_ref[...] = buf[slot].astype(BF16)

    @pl.when(jnp.logical_not(valid))
    def _():
        o_ref[...] = jnp.zeros(o_ref.shape, BF16)


def _gather_rows(nvalid_blocks, tok_of_row, h):
    n_rows = tok_of_row.shape[0]
    d = h.shape[1]
    nb = n_rows // GATHER_BM
    tok3 = tok_of_row.reshape(nb, 1, GATHER_BM)
    idx_block = (1, 1, GATHER_BM)
    return pl.pallas_call(
        _gather_kernel,
        grid_spec=pltpu.PrefetchScalarGridSpec(
            num_scalar_prefetch=1,
            grid=(nb,),
            in_specs=[
                pl.BlockSpec(idx_block, lambda i, nv: (i, 0, 0), memory_space=pltpu.SMEM),
                pl.BlockSpec(idx_block, lambda i, nv: (jnp.minimum(i + 1, nb - 1), 0, 0), memory_space=pltpu.SMEM),
                pl.BlockSpec(memory_space=pl.ANY),
            ],
            out_specs=pl.BlockSpec((GATHER_BM, d), lambda i, nv: (i, 0)),
            scratch_shapes=[pltpu.VMEM((2, GATHER_BM, d), F32), pltpu.SemaphoreType.DMA((2,))],
        ),
        out_shape=jax.ShapeDtypeStruct((n_rows, d), BF16),
        compiler_params=_cparams(("arbitrary",)),
        name="moe_gather",
    )(nvalid_blocks, tok3, tok3, h)


def _combine_kernel(pos_ref, pos_next_ref, x_ref, gate_ref, y_hbm, o_ref, buf, sem):
    i = pl.program_id(0)
    nb = pl.num_programs(0)
    n = o_ref.shape[0]
    slot = i % 2

    def row_copy(idx_ref, r, k, s):
        return pltpu.make_async_copy(y_hbm.at[pl.ds(idx_ref[0, k, r], 1)], buf.at[s, k, pl.ds(r, 1)], sem.at[s])

    def start_all(idx_ref, s):
        def start(r, c):
            row_copy(idx_ref, r, 0, s).start()
            row_copy(idx_ref, r, 1, s).start()
            return c
        lax.fori_loop(0, n, start, 0, unroll=DMA_UNROLL)

    @pl.when(i == 0)
    def _():
        start_all(pos_ref, 0)

    @pl.when(i + 1 < nb)
    def _():
        start_all(pos_next_ref, 1 - slot)

    def wait(r, c):
        row_copy(pos_ref, r, 0, slot).wait()
        row_copy(pos_ref, r, 1, slot).wait()
        return c
    lax.fori_loop(0, n, wait, 0, unroll=DMA_UNROLL)
    g = gate_ref[...]
    o_ref[...] = x_ref[...] + g[:, 0:1] * buf[slot, 0] + g[:, 1:2] * buf[slot, 1]


def _combine(pos, x, gate, y):
    m, d = x.shape
    nb = m // GATHER_BM
    pos3 = pos.reshape(nb, GATHER_BM, TOP_K).transpose(0, 2, 1)
    idx_block = (1, TOP_K, GATHER_BM)
    return pl.pallas_call(
        _combine_kernel,
        grid=(nb,),
        in_specs=[
            pl.BlockSpec(idx_block, lambda i: (i, 0, 0), memory_space=pltpu.SMEM),
            pl.BlockSpec(idx_block, lambda i: (jnp.minimum(i + 1, nb - 1), 0, 0), memory_space=pltpu.SMEM),
            pl.BlockSpec((GATHER_BM, d), lambda i: (i, 0)),
            pl.BlockSpec((GATHER_BM, LANES), lambda i: (i, 0)),
            pl.BlockSpec(memory_space=pl.ANY),
        ],
        out_specs=pl.BlockSpec((GATHER_BM, d), lambda i: (i, 0)),
        out_shape=jax.ShapeDtypeStruct((m, d), F32),
        scratch_shapes=[pltpu.VMEM((2, TOP_K, GATHER_BM, d), F32), pltpu.SemaphoreType.DMA((2,))],
        compiler_params=_cparams(("arbitrary",)),
        name="moe_combine",
    )(pos3, pos3, x, gate, y)


def _dispatch_plan(idx, n_tokens):
    n_assign = n_tokens * TOP_K
    n_tiles = n_assign // MOE_TILE + N_EXPERTS
    e_flat = idx.reshape(n_assign)
    onehot = (e_flat[:, None] == jnp.arange(N_EXPERTS, dtype=I32)[None, :]).astype(I32)
    counts = jnp.sum(onehot, axis=0)
    rank = jnp.take_along_axis(jnp.cumsum(onehot, axis=0) - onehot, e_flat[:, None], axis=1)[:, 0]
    tiles_e = (counts + MOE_TILE - 1) // MOE_TILE
    tile_end = jnp.cumsum(tiles_e)
    tile_start = tile_end - tiles_e
    total = tile_end[-1]
    pos = tile_start[e_flat] * MOE_TILE + rank
    t = jnp.arange(n_tiles, dtype=I32)
    t_eff = jnp.minimum(t, total - 1)
    eid = jnp.minimum(jnp.sum((t_eff[:, None] >= tile_end[None, :]).astype(I32), axis=1), N_EXPERTS - 1)
    rows_valid = jnp.clip(counts[eid] - (t_eff - tile_start[eid]) * MOE_TILE, 0, MOE_TILE)
    rows_valid = jnp.where(t < total, rows_valid, 0)
    nsub = (rows_valid + FFN_SUB - 1) // FFN_SUB
    sub_per_tile = MOE_TILE // GATHER_BM
    sub_id = jnp.arange(n_tiles * sub_per_tile, dtype=I32)
    nvalid_blocks = ((sub_id % sub_per_tile) < nsub[sub_id // sub_per_tile]).astype(I32)
    tok_of_row = jnp.zeros((n_tiles * MOE_TILE,), I32).at[pos].set(jnp.arange(n_assign, dtype=I32) // TOP_K)
    meta = (eid, nsub.astype(I32), t_eff, jnp.reshape(total, (1,)).astype(I32))
    return meta, nvalid_blocks, tok_of_row, pos.reshape(n_tokens, TOP_K)


def _moe(x, g, rw, w1_all, w3_all, w2_all, layer_idx):
    m, d = x.shape
    w1 = w1_all.reshape((-1,) + w1_all.shape[2:])
    w3 = w3_all.reshape((-1,) + w3_all.shape[2:])
    w2 = w2_all.reshape((-1,) + w2_all.shape[2:])
    rw_padded = jnp.zeros((d, LANES), F32).at[:, :N_EXPERTS].set(rw)
    h, idx, gate = _route(x, g, rw_padded)
    meta, nvalid_blocks, tok_of_row, pos = _dispatch_plan(idx[:, :TOP_K], m)
    meta = (meta[0] + layer_idx * N_EXPERTS,) + meta[1:]
    xs = _gather_rows(nvalid_blocks, tok_of_row, h)
    y = _swiglu(meta, xs, w1, w3, w2, None, MOE_TILE)
    return _combine(pos, x, gate, y)


def _dense(x, g, w1_all, w3_all, w2_all, layer_idx):
    m, d = x.shape
    n_tiles = m // DENSE_TILE
    meta = (jnp.full((n_tiles,), layer_idx, I32), jnp.full((n_tiles,), DENSE_TILE // FFN_SUB, I32),
            jnp.arange(n_tiles, dtype=I32), jnp.full((1,), n_tiles, I32))
    return _swiglu(meta, x, w1_all, w3_all, w2_all, g, DENSE_TILE)


def kernel(x, positions, attn_norm, w_in, q_norm, k_norm, lambda_q1, lambda_k1, lambda_q2, lambda_k2, subln, pool_w, pool_b, pool_scale, w_out, ffn_norm, dense_w1, dense_w3, dense_w2, router_w, moe_w1, moe_w3, moe_w2):
    batch, seq, d = x.shape
    depth = w_in.shape[0]
    m = batch * seq
    xf = x.reshape(m, d)
    pos = positions.reshape(m, 1)
    inv_freq = 1.0 / (ROPE_THETA ** (jnp.arange(0, HEAD_DIM, 2, dtype=F32) / HEAD_DIM))
    invf = jnp.tile(inv_freq, LANES // (HEAD_DIM // 2))[None, :]
    for l in range(depth):
        qk, v, u = _in_proj(xf, attn_norm[l][None, :], w_in, l)
        gain = jnp.concatenate([jnp.tile(q_norm[l], ATTN_WIDTH // HEAD_DIM),
                                jnp.tile(k_norm[l], ATTN_WIDTH // HEAD_DIM)])[None, :]
        qk = _qk_rope(pos, invf, qk, gain)
        lambda_init = 0.8 - 0.6 * math.exp(-0.3 * l)
        a = _attention(jnp.full((1,), lambda_init, F32), qk, v, lambda_q1[l][None, :], lambda_k1[l][None, :],
                       lambda_q2[l][None, :], lambda_k2[l][None, :], subln[l][:, None], batch, seq)
        p = _pool(u, pool_w, l, pool_b[l].reshape(1, POOL_WIDTH), pool_scale[l][None, :], seq)
        xf = _out_proj(a, p, w_out, l, xf)
        i = l // 2
        if l % 2 == 0:
            xf = _dense(xf, ffn_norm[l][None, :], dense_w1, dense_w3, dense_w2, i)
        else:
            xf = _moe(xf, ffn_norm[l][None, :], router_w[i], moe_w1, moe_w3, moe_w2, i)
    return xf.reshape(batch, seq, d)
```

```python
import functools
import math

import jax
import jax.numpy as jnp
from jax import lax
from jax.experimental import pallas as pl
from jax.experimental.pallas import tpu as pltpu

F32 = jnp.float32
BF16 = jnp.bfloat16
I32 = jnp.int32

D_MODEL = 2048
ATTN_WIDTH = 1024
POOL_WIDTH = 1024
HEAD_DIM = 64
N_HEADS = 8
HEAD_WIDTH = 2 * HEAD_DIM
POOL_WINDOWS = (2, 4, 8, 16)
POOL_GROUP_DIM = 256
ROPE_THETA = 10000.0
NORM_EPS = 1e-6
N_EXPERTS = 8
TOP_K = 2
NEG_BIG = -1e30
LOG2_E = math.log2(math.e)

LANES = 128
VMEM_BUDGET = 56 * 1024 * 1024

PROJ_BM = 2048
PROJ_BN = 512
IN_PROJ_BM = 2048
IN_PROJ_BN = 512
X_CHUNK = 256
NORM_CHUNK = 128
ROPE_BM = 256
ATTN_BQ = 512
ATTN_STREAMS = 2
POOL_BM = 256
FFN_SUB = 256
FFN_MM = 512
FFN_TRIP = 4
FFN_BF = 256
DENSE_TILE = 2048
MOE_TILE = 2560
GATHER_BM = 256
DMA_UNROLL = 8


def _cparams(sem, vmem=None):
    return pltpu.CompilerParams(dimension_semantics=sem, vmem_limit_bytes=vmem)


def _split_bf16(x):
    hi = x.astype(BF16)
    lo = (x - hi.astype(F32)).astype(BF16)
    return hi, lo


def _rms_rows(x_ref, g_ref, h_ref, rows):
    def body(c, carry):
        r = pl.multiple_of(c * NORM_CHUNK, NORM_CHUNK)
        x = x_ref[pl.ds(r, NORM_CHUNK), :]
        ms = jnp.mean(x * x, axis=-1, keepdims=True)
        h_ref[pl.ds(r, NORM_CHUNK), :] = (x * lax.rsqrt(ms + NORM_EPS) * g_ref[...]).astype(h_ref.dtype)
        return carry
    lax.fori_loop(0, rows // NORM_CHUNK, body, 0)


def _in_proj_kernel(x_hbm, g_ref, w_ref, qk_ref, v_ref, u_ref, h_ref, xbuf, sem, *, n_qk, n_v):
    i = pl.program_id(0)
    j = pl.program_id(1)
    bm = h_ref.shape[0]
    chunk = xbuf.shape[1]
    n_chunks = bm // chunk

    @pl.when(j == 0)
    def _():
        def chunk_copy(c, slot):
            rows = pl.ds(pl.multiple_of(i * bm + c * chunk, chunk), chunk)
            return pltpu.make_async_copy(x_hbm.at[rows], xbuf.at[slot], sem.at[slot])

        chunk_copy(0, 0).start()
        for c in range(n_chunks):
            slot = c % 2
            if c + 1 < n_chunks:
                chunk_copy(c + 1, 1 - slot).start()
            chunk_copy(c, slot).wait()
            _rms_rows(xbuf.at[slot], g_ref, h_ref.at[pl.ds(c * chunk, chunk)], chunk)

    def proj():
        return jnp.dot(h_ref[...], w_ref[...].astype(BF16), preferred_element_type=F32)

    @pl.when(j < n_qk)
    def _():
        qk_ref[...] = proj()

    @pl.when((j >= n_qk) & (j < n_qk + n_v))
    def _():
        v_ref[...] = proj().astype(BF16)

    @pl.when(j >= n_qk + n_v)
    def _():
        u_ref[...] = proj()


def _in_proj(x, g, w_all, layer):
    m, d = x.shape
    bn = IN_PROJ_BN
    bm = IN_PROJ_BM
    n_qk = 2 * ATTN_WIDTH // bn
    n_v = ATTN_WIDTH // bn
    n_u = POOL_WIDTH // bn
    assert w_all.shape[2] == (n_qk + n_v + n_u) * bn and m % bm == 0
    return pl.pallas_call(
        functools.partial(_in_proj_kernel, n_qk=n_qk, n_v=n_v),
        grid=(m // bm, n_qk + n_v + n_u),
        in_specs=[
            pl.BlockSpec(memory_space=pl.ANY),
            pl.BlockSpec((1, d), lambda i, j: (0, 0)),
            pl.BlockSpec((None, d, bn), lambda i, j: (layer, 0, j)),
        ],
        out_specs=[
            pl.BlockSpec((bm, bn), lambda i, j: (i, jnp.minimum(j, n_qk - 1))),
            pl.BlockSpec((bm, bn), lambda i, j: (i, jnp.clip(j - n_qk, 0, n_v - 1))),
            pl.BlockSpec((bm, bn), lambda i, j: (i, jnp.clip(j - n_qk - n_v, 0, n_u - 1))),
        ],
        out_shape=[
            jax.ShapeDtypeStruct((m, 2 * ATTN_WIDTH), F32),
            jax.ShapeDtypeStruct((m, ATTN_WIDTH), BF16),
            jax.ShapeDtypeStruct((m, POOL_WIDTH), F32),
        ],
        scratch_shapes=[
            pltpu.VMEM((bm, d), BF16),
            pltpu.VMEM((2, X_CHUNK, d), F32),
            pltpu.SemaphoreType.DMA((2,)),
        ],
        compiler_params=_cparams(("arbitrary", "arbitrary"), VMEM_BUDGET),
        name="in_proj",
    )(x, g, w_all)


def _qk_rope_kernel(pos_ref, invf_ref, qk_ref, gain_ref, o_ref):
    bm = qk_ref.shape[0]
    ang = pos_ref[...].astype(F32) * invf_ref[...]
    cos = jnp.cos(ang)
    sin = jnp.sin(ang)
    lane = lax.broadcasted_iota(I32, (bm, LANES), 1)
    first_half = (lane % HEAD_DIM) < (HEAD_DIM // 2)
    sin_signed = jnp.where(first_half, -sin, sin)
    gr = lax.broadcasted_iota(I32, (LANES, LANES), 0) // HEAD_DIM
    gc = lax.broadcasted_iota(I32, (LANES, LANES), 1) // HEAD_DIM
    group_ones = jnp.where(gr == gc, 1.0, 0.0).astype(BF16)
    n_blocks = qk_ref.shape[1] // LANES
    for hb in range(n_blocks):
        cols = slice(hb * LANES, (hb + 1) * LANES)
        x = qk_ref[:, cols]
        hi, lo = _split_bf16(x * x)
        ssum = (jnp.dot(hi, group_ones, preferred_element_type=F32)
                + jnp.dot(lo, group_ones, preferred_element_type=F32))
        y = x * lax.rsqrt(ssum * (1.0 / HEAD_DIM) + NORM_EPS) * gain_ref[:, cols]
        swapped = jnp.where(first_half, pltpu.roll(y, LANES - HEAD_DIM // 2, 1),
                            pltpu.roll(y, HEAD_DIM // 2, 1))
        r = y * cos + swapped * sin_signed
        if hb < n_blocks // 2:
            r = r * (HEAD_DIM ** -0.5 * LOG2_E)
        o_ref[:, cols] = r.astype(BF16)


def _qk_rope(pos, invf, qk, gain):
    m, w = qk.shape
    return pl.pallas_call(
        _qk_rope_kernel,
        grid=(m // ROPE_BM,),
        in_specs=[
            pl.BlockSpec((ROPE_BM, 1), lambda i: (i, 0)),
            pl.BlockSpec((1, LANES), lambda i: (0, 0)),
            pl.BlockSpec((ROPE_BM, w), lambda i: (i, 0)),
            pl.BlockSpec((1, w), lambda i: (0, 0)),
        ],
        out_specs=pl.BlockSpec((ROPE_BM, w), lambda i: (i, 0)),
        out_shape=jax.ShapeDtypeStruct((m, w), BF16),
        compiler_params=_cparams(("arbitrary",)),
        name="qk_rope",
    )(pos, invf, qk, gain)


def _block_streams(nq):
    chains = [[(qi, kb) for kb in range(qi + 1)] for qi in range(nq - 1, -1, -1)]
    streams = tuple([] for _ in range(ATTN_STREAMS))
    for chain in chains:
        min(streams, key=len).extend(chain)
    order = []
    for i in range(max(len(s) for s in streams)):
        order.extend(s[i] for s in streams if i < len(s))
    return order


def _attn_kernel(linit_ref, q_ref, k_ref, v_ref, lq1_ref, lk1_ref, lq2_ref, lk2_ref, sg_ref, o_ref,
                 q2_ref, vt_ref, m_ref, l_ref, acc_ref):
    nq, _, bq = vt_ref.shape
    lane = lax.broadcasted_iota(I32, (bq, HEAD_WIDTH), 1)
    for c in range(nq):
        rows = slice(c * bq, (c + 1) * bq)
        vt_ref[c] = v_ref[rows, :].astype(F32).T.astype(BF16)
        q = q_ref[rows, :]
        zero = jnp.zeros_like(q)
        q2_ref[c, 0:bq, :] = jnp.where(lane < HEAD_DIM, q, zero)
        q2_ref[c, bq:2 * bq, :] = jnp.where(lane >= HEAD_DIM, q, zero)
    m_ref[...] = jnp.full(m_ref.shape, NEG_BIG, F32)
    l_ref[...] = jnp.zeros(l_ref.shape, F32)
    acc_ref[...] = jnp.zeros(acc_ref.shape, F32)

    for qi, kb in _block_streams(nq):
        k = k_ref[kb * bq:(kb + 1) * bq, :]
        s = lax.dot_general(k, q2_ref[qi], (((1,), (1,)), ((), ())), preferred_element_type=F32)
        if kb == qi:
            key = lax.broadcasted_iota(I32, s.shape, 0)
            qry = lax.broadcasted_iota(I32, s.shape, 1) & (bq - 1)
            s = jnp.where(key <= qry, s, NEG_BIG)
        m_old = m_ref[qi]
        m_new = jnp.maximum(m_old, jnp.max(s, axis=0, keepdims=True))
        alpha = jnp.exp2(m_old - m_new)
        p = jnp.exp2(s - m_new)
        l_ref[qi] = alpha * l_ref[qi] + jnp.sum(p, axis=0, keepdims=True)
        acc_ref[qi] = alpha * acc_ref[qi] + jnp.dot(vt_ref[kb], p.astype(BF16), preferred_element_type=F32)
        m_ref[qi] = m_new

    lambda_init = linit_ref[0]
    lam = (jnp.exp(jnp.sum(lq1_ref[...] * lk1_ref[...], axis=-1, keepdims=True))
           - jnp.exp(jnp.sum(lq2_ref[...] * lk2_ref[...], axis=-1, keepdims=True))
           + lambda_init)
    for qi in range(nq):
        acc = acc_ref[qi]
        l = l_ref[qi]
        o = acc[:, :bq] / l[:, :bq] - lam * (acc[:, bq:] / l[:, bq:])
        ms = jnp.mean(o * o, axis=0, keepdims=True)
        o = o * lax.rsqrt(ms + NORM_EPS) * sg_ref[...] * (1.0 - lambda_init)
        o_ref[qi * bq:(qi + 1) * bq, :] = o.T.astype(BF16)


def _attention(linit, qk, v, lq1, lk1, lq2, lk2, sg_col, batch, seq):
    m = qk.shape[0]
    nq = seq // ATTN_BQ
    vec = pl.BlockSpec((1, HEAD_DIM), lambda b, h: (0, 0))
    return pl.pallas_call(
        _attn_kernel,
        grid=(batch, N_HEADS),
        in_specs=[
            pl.BlockSpec(memory_space=pltpu.SMEM),
            pl.BlockSpec((seq, HEAD_WIDTH), lambda b, h: (b, h)),
            pl.BlockSpec((seq, HEAD_WIDTH), lambda b, h: (b, N_HEADS + h)),
            pl.BlockSpec((seq, HEAD_WIDTH), lambda b, h: (b, h)),
            vec, vec, vec, vec,
            pl.BlockSpec((HEAD_WIDTH, 1), lambda b, h: (0, 0)),
        ],
        out_specs=pl.BlockSpec((seq, HEAD_WIDTH), lambda b, h: (b, h)),
        out_shape=jax.ShapeDtypeStruct((m, ATTN_WIDTH), BF16),
        scratch_shapes=[
            pltpu.VMEM((nq, 2 * ATTN_BQ, HEAD_WIDTH), BF16),
            pltpu.VMEM((nq, HEAD_WIDTH, ATTN_BQ), BF16),
            pltpu.VMEM((nq, 1, 2 * ATTN_BQ), F32),
            pltpu.VMEM((nq, 1, 2 * ATTN_BQ), F32),
            pltpu.VMEM((nq, HEAD_WIDTH, 2 * ATTN_BQ), F32),
        ],
        compiler_params=_cparams(("arbitrary", "arbitrary")),
        name="diff_attn",
    )(linit, qk, qk, v, lq1, lk1, lq2, lk2, sg_col)


def _pool_kernel(uc_ref, up_ref, w_ref, b_ref, sc_ref, o_ref, *, chunks_per_seq):
    c = pl.program_id(0) % chunks_per_seq
    bm = uc_ref.shape[0]
    row = lax.broadcasted_iota(I32, (bm, bm), 0)
    col = lax.broadcasted_iota(I32, (bm, bm), 1)
    t = c * bm + lax.broadcasted_iota(I32, (bm, 1), 0)
    has_prev = c > 0
    for g, win in enumerate(POOL_WINDOWS):
        cols = slice(g * POOL_GROUP_DIM, (g + 1) * POOL_GROUP_DIM)
        cur = jnp.where((row >= col) & (row - col < win), 1.0, 0.0).astype(BF16)
        prv = jnp.where(col - row > bm - win, 1.0, 0.0).astype(BF16)
        u = uc_ref[:, cols]
        u_hi, u_lo = _split_bf16(u)
        p_hi, p_lo = _split_bf16(up_ref[:, cols])
        wsum = (jnp.dot(cur, u_hi, preferred_element_type=F32)
                + jnp.dot(cur, u_lo, preferred_element_type=F32))
        wprev = (jnp.dot(prv, p_hi, preferred_element_type=F32)
                 + jnp.dot(prv, p_lo, preferred_element_type=F32))
        wsum = wsum + jnp.where(has_prev, wprev, 0.0)
        cnt = jnp.minimum(t + 1, win).astype(F32)
        d = wsum / cnt - u
        y = jnp.dot(d.astype(BF16), w_ref[g].astype(BF16), preferred_element_type=F32) + b_ref[:, cols]
        o_ref[:, cols] = (y * sc_ref[:, cols]).astype(BF16)


def _pool(u, w_all, layer, b, sc, seq):
    m, width = u.shape
    cps = seq // POOL_BM
    return pl.pallas_call(
        functools.partial(_pool_kernel, chunks_per_seq=cps),
        grid=(m // POOL_BM,),
        in_specs=[
            pl.BlockSpec((POOL_BM, width), lambda i: (i, 0)),
            pl.BlockSpec((POOL_BM, width), lambda i: (jnp.maximum(i - 1, 0), 0)),
            pl.BlockSpec((None,) + w_all.shape[1:], lambda i: (layer, 0, 0, 0)),
            pl.BlockSpec((1, width), lambda i: (0, 0)),
            pl.BlockSpec((1, width), lambda i: (0, 0)),
        ],
        out_specs=pl.BlockSpec((POOL_BM, width), lambda i: (i, 0)),
        out_shape=jax.ShapeDtypeStruct((m, width), BF16),
        compiler_params=_cparams(("arbitrary",)),
        name="pool_mixer",
    )(u, u, w_all, b, sc)


def _out_proj_kernel(a_ref, p_ref, wa_ref, wp_ref, x_ref, o_ref):
    y = jnp.dot(a_ref[...], wa_ref[...].astype(BF16), preferred_element_type=F32)
    y = y + jnp.dot(p_ref[...], wp_ref[...].astype(BF16), preferred_element_type=F32)
    o_ref[...] = x_ref[...] + y


def _out_proj(a, p, w_all, layer, x):
    m, d = x.shape
    ka = a.shape[1]
    kp = p.shape[1]
    assert ka == kp
    return pl.pallas_call(
        _out_proj_kernel,
        grid=(m // PROJ_BM, d // PROJ_BN),
        in_specs=[
            pl.BlockSpec((PROJ_BM, ka), lambda i, j: (i, 0)),
            pl.BlockSpec((PROJ_BM, kp), lambda i, j: (i, 0)),
            pl.BlockSpec((None, ka, PROJ_BN), lambda i, j: (layer, 0, j)),
            pl.BlockSpec((None, kp, PROJ_BN), lambda i, j: (layer, 1, j)),
            pl.BlockSpec((PROJ_BM, PROJ_BN), lambda i, j: (i, j)),
        ],
        out_specs=pl.BlockSpec((PROJ_BM, PROJ_BN), lambda i, j: (i, j)),
        out_shape=jax.ShapeDtypeStruct((m, d), F32),
        compiler_params=_cparams(("arbitrary", "arbitrary"), VMEM_BUDGET),
        name="out_proj",
    )(a, p, w_all, w_all, x)


def _route_kernel(x_ref, g_ref, rw_ref, h_ref, idx_ref, gate_ref):
    x = x_ref[...]
    ms = jnp.mean(x * x, axis=-1, keepdims=True)
    h = x * lax.rsqrt(ms + NORM_EPS) * g_ref[...]
    h_hi, h_lo = _split_bf16(h)
    half = h.shape[1] // 2
    bits = lax.bitcast_convert_type(h_hi.astype(F32), jnp.uint32)
    h_ref[...] = (bits[:, half:] & jnp.uint32(0xFFFF0000)) | (bits[:, :half] >> 16)
    w_hi, w_lo = _split_bf16(rw_ref[...])
    logits = (jnp.dot(h_hi, w_hi, preferred_element_type=F32)
              + jnp.dot(h_hi, w_lo, preferred_element_type=F32)
              + jnp.dot(h_lo, w_hi, preferred_element_type=F32))
    lane = lax.broadcasted_iota(I32, logits.shape, 1)
    logits = jnp.where(lane < N_EXPERTS, logits, NEG_BIG)
    v1 = jnp.max(logits, axis=-1, keepdims=True)
    i1 = jnp.min(jnp.where(logits == v1, lane, LANES), axis=-1, keepdims=True)
    rest = jnp.where(lane == i1, NEG_BIG, logits)
    v2 = jnp.max(rest, axis=-1, keepdims=True)
    i2 = jnp.min(jnp.where(rest == v2, lane, LANES), axis=-1, keepdims=True)
    e = jnp.exp(v2 - v1)
    g1 = 1.0 / (1.0 + e)
    g2 = e / (1.0 + e)
    idx_ref[...] = jnp.where(lane == 0, i1, jnp.where(lane == 1, i2, 0))
    gate_ref[...] = jnp.where(lane == 0, g1, jnp.where(lane == 1, g2, 0.0))


def _route(x, g, rw_padded):
    m, d = x.shape
    bm = 256
    return pl.pallas_call(
        _route_kernel,
        grid=(m // bm,),
        in_specs=[
            pl.BlockSpec((bm, d), lambda i: (i, 0)),
            pl.BlockSpec((1, d), lambda i: (0, 0)),
            pl.BlockSpec((d, LANES), lambda i: (0, 0)),
        ],
        out_specs=[
            pl.BlockSpec((bm, d // 2), lambda i: (i, 0)),
            pl.BlockSpec((bm, LANES), lambda i: (i, 0)),
            pl.BlockSpec((bm, LANES), lambda i: (i, 0)),
        ],
        out_shape=[
            jax.ShapeDtypeStruct((m, d // 2), jnp.uint32),
            jax.ShapeDtypeStruct((m, LANES), I32),
            jax.ShapeDtypeStruct((m, LANES), F32),
        ],
        compiler_params=_cparams(("arbitrary",)),
        name="ffn_norm_route",
    )(x, g, rw_padded)


def _swiglu_kernel(eid_ref, nsub_ref, blk_ref, tot_ref, x_ref, w1_ref, w3_ref, w2_ref, *rest, has_res):
    if has_res:
        gain_ref, o_ref, w1s, w3s, w2s, h_ref, sem = rest
    else:
        o_ref, w1s, w3s, w2s = rest
        h_ref = x_ref
    t = pl.program_id(0)
    j = pl.program_id(1)
    nsub = nsub_ref[t]
    tile_rows = o_ref.shape[0]

    @pl.when(j == 0)
    def _():
        if has_res:
            seed = pltpu.make_async_copy(
                x_ref.at[pl.ds(pl.multiple_of(t * tile_rows, tile_rows), tile_rows)], o_ref, sem)
            seed.start()
            seed.wait()
            _rms_rows(o_ref, gain_ref, h_ref, tile_rows)
        else:
            def init(i, carry):
                rows = pl.ds(pl.multiple_of(i * FFN_SUB, FFN_SUB), FFN_SUB)
                o_ref[rows, :] = jnp.zeros((FFN_SUB, o_ref.shape[1]), F32)
                return carry
            lax.fori_loop(0, tile_rows // FFN_SUB, init, 0)

    def row_block(start, n_rows):
        rows = pl.ds(pl.multiple_of(start, FFN_SUB), n_rows)
        xs = h_ref[rows, :]
        a = jnp.dot(xs, w1s[...], preferred_element_type=F32)
        b = jnp.dot(xs, w3s[...], preferred_element_type=F32)
        hidden = (a * jax.nn.sigmoid(a) * b).astype(BF16)
        o_ref[rows, :] += jnp.dot(hidden, w2s[...], preferred_element_type=F32)

    def cast_weights():
        w1s[...] = w1_ref[0].astype(BF16)
        w3s[...] = w3_ref[0].astype(BF16)
        w2s[...] = w2_ref[0].astype(BF16)

    if has_res:
        cast_weights()
        for i in range(o_ref.shape[0] // FFN_MM):
            row_block(i * FFN_MM, FFN_MM)
    else:
        @pl.when(nsub > 0)
        def _():
            cast_weights()
            n_mm = nsub // (FFN_MM // FFN_SUB)
            n_trips = n_mm // FFN_TRIP

            def trip(i, carry):
                for k in range(FFN_TRIP):
                    row_block((i * FFN_TRIP + k) * FFN_MM, FFN_MM)
                return carry
            lax.fori_loop(0, n_trips, trip, 0)

            def single(i, carry):
                row_block(i * FFN_MM, FFN_MM)
                return carry
            lax.fori_loop(n_trips * FFN_TRIP, n_mm, single, 0)

            @pl.when(nsub % (FFN_MM // FFN_SUB) == 1)
            def _():
                row_block((nsub - 1) * FFN_SUB, FFN_SUB)


def _swiglu(meta, x, w1, w3, w2, gain, tile_rows):
    eid, nsub, blk, tot = meta
    n_tiles = eid.shape[0]
    d = x.shape[1]
    f = w1.shape[2]
    n_j = f // FFN_BF
    has_res = gain is not None

    def row_map(t, j, eid, nsub, blk, tot):
        return (blk[t], 0)

    def out_map(t, j, eid, nsub, blk, tot):
        return (t, 0)

    def hidden_block(t, j, tot):
        return jnp.where(t < tot[0], j, n_j - 1)

    def w13_map(t, j, eid, nsub, blk, tot):
        return (eid[t], 0, hidden_block(t, j, tot))

    def w2_map(t, j, eid, nsub, blk, tot):
        return (eid[t], hidden_block(t, j, tot), 0)

    once = pl.Buffered(1)
    in_specs = [
        pl.BlockSpec(memory_space=pl.ANY) if has_res else pl.BlockSpec((tile_rows, d), row_map, pipeline_mode=once),
        pl.BlockSpec((1, d, FFN_BF), w13_map),
        pl.BlockSpec((1, d, FFN_BF), w13_map),
        pl.BlockSpec((1, FFN_BF, d), w2_map),
    ]
    args = [x, w1, w3, w2]
    scratch = [
        pltpu.VMEM((d, FFN_BF), BF16),
        pltpu.VMEM((d, FFN_BF), BF16),
        pltpu.VMEM((FFN_BF, d), BF16),
    ]
    if has_res:
        in_specs.append(pl.BlockSpec((1, d), lambda t, j, *_: (0, 0)))
        args.append(gain)
        scratch.append(pltpu.VMEM((tile_rows, d), BF16))
        scratch.append(pltpu.SemaphoreType.DMA(()))
    return pl.pallas_call(
        functools.partial(_swiglu_kernel, has_res=has_res),
        grid_spec=pltpu.PrefetchScalarGridSpec(
            num_scalar_prefetch=4,
            grid=(n_tiles, n_j),
            in_specs=in_specs,
            out_specs=pl.BlockSpec((tile_rows, d), out_map, pipeline_mode=once),
            scratch_shapes=scratch,
        ),
        out_shape=jax.ShapeDtypeStruct((n_tiles * tile_rows, d), F32),
        compiler_params=_cparams(("arbitrary", "arbitrary"), VMEM_BUDGET),
        name="swiglu_res" if has_res else "swiglu_moe",
    )(eid, nsub, blk, tot, *args)


def _gather_kernel(nvalid_ref, tok_ref, tok_next_ref, h_hbm, o_ref, buf, sem):
    i = pl.program_id(0)
    nb = pl.num_programs(0)
    n = o_ref.shape[0]
    slot = i % 2
    nxt = jnp.minimum(i + 1, nb - 1)

    def row_copy(idx_ref, r, s):
        return pltpu.make_async_copy(h_hbm.at[pl.ds(idx_ref[0, 0, r], 1)], buf.at[s, pl.ds(r, 1)], sem.at[s])

    def start_all(idx_ref, s):
        def start(r, c):
            row_copy(idx_ref, r, s).start()
            return c
        lax.fori_loop(0, n, start, 0, unroll=DMA_UNROLL)

    @pl.when((i == 0) & (nvalid_ref[0] > 0))
    def _():
        start_all(tok_ref, 0)

    @pl.when((i + 1 < nb) & (nvalid_ref[nxt] > 0))
    def _():
        start_all(tok_next_ref, 1 - slot)

    valid = nvalid_ref[i] > 0

    @pl.when(valid)
    def _():
        def wait(r, c):
            row_copy(tok_ref, r, slot).wait()
            return c
        lax.fori_loop(0, n, wait, 0, unroll=DMA_UNROLL)
        words = buf[slot]
        half = words.shape[1]
        o_ref[:, :half] = lax.bitcast_convert_type(words << 16, F32).astype(BF16)
        o_ref[:, half:] = lax.bitcast_convert_type(words & jnp.uint32(0xFFFF0000), F32).astype(BF16)

    @pl.when(jnp.logical_not(valid))
    def _():
        o_ref[...] = jnp.zeros(o_ref.shape, BF16)


def _gather_rows(nvalid_blocks, tok_of_row, h):
    n_rows = tok_of_row.shape[0]
    d = h.shape[1]
    nb = n_rows // GATHER_BM
    tok3 = tok_of_row.reshape(nb, 1, GATHER_BM)
    idx_block = (1, 1, GATHER_BM)
    return pl.pallas_call(
        _gather_kernel,
        grid_spec=pltpu.PrefetchScalarGridSpec(
            num_scalar_prefetch=1,
            grid=(nb,),
            in_specs=[
                pl.BlockSpec(idx_block, lambda i, nv: (i, 0, 0), memory_space=pltpu.SMEM),
                pl.BlockSpec(idx_block, lambda i, nv: (jnp.minimum(i + 1, nb - 1), 0, 0), memory_space=pltpu.SMEM),
                pl.BlockSpec(memory_space=pl.ANY),
            ],
            out_specs=pl.BlockSpec((GATHER_BM, 2 * d), lambda i, nv: (i, 0)),
            scratch_shapes=[pltpu.VMEM((2, GATHER_BM, d), jnp.uint32), pltpu.SemaphoreType.DMA((2,))],
        ),
        out_shape=jax.ShapeDtypeStruct((n_rows, 2 * d), BF16),
        compiler_params=_cparams(("arbitrary",)),
        name="moe_gather",
    )(nvalid_blocks, tok3, tok3, h)


def _combine_kernel(pos_ref, pos_next_ref, x_ref, gate_ref, y_hbm, o_ref, buf, sem):
    i = pl.program_id(0)
    nb = pl.num_programs(0)
    n = o_ref.shape[0]
    slot = i % 2

    def row_copy(idx_ref, r, k, s):
        return pltpu.make_async_copy(y_hbm.at[pl.ds(idx_ref[0, k, r], 1)], buf.at[s, k, pl.ds(r, 1)], sem.at[s])

    def start_all(idx_ref, s):
        def start(r, c):
            row_copy(idx_ref, r, 0, s).start()
            row_copy(idx_ref, r, 1, s).start()
            return c
        lax.fori_loop(0, n, start, 0, unroll=DMA_UNROLL)

    @pl.when(i == 0)
    def _():
        start_all(pos_ref, 0)

    @pl.when(i + 1 < nb)
    def _():
        start_all(pos_next_ref, 1 - slot)

    def wait(r, c):
        row_copy(pos_ref, r, 0, slot).wait()
        row_copy(pos_ref, r, 1, slot).wait()
        return c
    lax.fori_loop(0, n, wait, 0, unroll=DMA_UNROLL)
    g = gate_ref[...]
    o_ref[...] = x_ref[...] + g[:, 0:1] * buf[slot, 0] + g[:, 1:2] * buf[slot, 1]


def _combine(pos, x, gate, y):
    m, d = x.shape
    nb = m // GATHER_BM
    pos3 = pos.reshape(nb, GATHER_BM, TOP_K).transpose(0, 2, 1)
    idx_block = (1, TOP_K, GATHER_BM)
    return pl.pallas_call(
        _combine_kernel,
        grid=(nb,),
        in_specs=[
            pl.BlockSpec(idx_block, lambda i: (i, 0, 0), memory_space=pltpu.SMEM),
            pl.BlockSpec(idx_block, lambda i: (jnp.minimum(i + 1, nb - 1), 0, 0), memory_space=pltpu.SMEM),
            pl.BlockSpec((GATHER_BM, d), lambda i: (i, 0)),
            pl.BlockSpec((GATHER_BM, LANES), lambda i: (i, 0)),
            pl.BlockSpec(memory_space=pl.ANY),
        ],
        out_specs=pl.BlockSpec((GATHER_BM, d), lambda i: (i, 0)),
        out_shape=jax.ShapeDtypeStruct((m, d), F32),
        scratch_shapes=[pltpu.VMEM((2, TOP_K, GATHER_BM, d), F32), pltpu.SemaphoreType.DMA((2,))],
        compiler_params=_cparams(("arbitrary",)),
        name="moe_combine",
    )(pos3, pos3, x, gate, y)


def _dispatch_plan(idx, n_tokens):
    n_assign = n_tokens * TOP_K
    n_tiles = n_assign // MOE_TILE + N_EXPERTS
    e_flat = idx.reshape(n_assign)
    onehot = (e_flat[:, None] == jnp.arange(N_EXPERTS, dtype=I32)[None, :]).astype(I32)
    counts = jnp.sum(onehot, axis=0)
    rank = jnp.take_along_axis(jnp.cumsum(onehot, axis=0) - onehot, e_flat[:, None], axis=1)[:, 0]
    tiles_e = (counts + MOE_TILE - 1) // MOE_TILE
    tile_end = jnp.cumsum(tiles_e)
    tile_start = tile_end - tiles_e
    total = tile_end[-1]
    pos = tile_start[e_flat] * MOE_TILE + rank
    t = jnp.arange(n_tiles, dtype=I32)
    t_eff = jnp.minimum(t, total - 1)
    eid = jnp.minimum(jnp.sum((t_eff[:, None] >= tile_end[None, :]).astype(I32), axis=1), N_EXPERTS - 1)
    rows_valid = jnp.clip(counts[eid] - (t_eff - tile_start[eid]) * MOE_TILE, 0, MOE_TILE)
    rows_valid = jnp.where(t < total, rows_valid, 0)
    nsub = (rows_valid + FFN_SUB - 1) // FFN_SUB
    sub_per_tile = MOE_TILE // GATHER_BM
    sub_id = jnp.arange(n_tiles * sub_per_tile, dtype=I32)
    nvalid_blocks = ((sub_id % sub_per_tile) < nsub[sub_id // sub_per_tile]).astype(I32)
    tok_of_row = jnp.zeros((n_tiles * MOE_TILE,), I32).at[pos].set(jnp.arange(n_assign, dtype=I32) // TOP_K)
    meta = (eid, nsub.astype(I32), t_eff, jnp.reshape(total, (1,)).astype(I32))
    return meta, nvalid_blocks, tok_of_row, pos.reshape(n_tokens, TOP_K)


def _moe(x, g, rw, w1_all, w3_all, w2_all, layer_idx):
    m, d = x.shape
    w1 = w1_all.reshape((-1,) + w1_all.shape[2:])
    w3 = w3_all.reshape((-1,) + w3_all.shape[2:])
    w2 = w2_all.reshape((-1,) + w2_all.shape[2:])
    rw_padded = jnp.zeros((d, LANES), F32).at[:, :N_EXPERTS].set(rw)
    h, idx, gate = _route(x, g, rw_padded)
    meta, nvalid_blocks, tok_of_row, pos = _dispatch_plan(idx[:, :TOP_K], m)
    meta = (meta[0] + layer_idx * N_EXPERTS,) + meta[1:]
    xs = _gather_rows(nvalid_blocks, tok_of_row, h)
    y = _swiglu(meta, xs, w1, w3, w2, None, MOE_TILE)
    return _combine(pos, x, gate, y)


def _dense(x, g, w1_all, w3_all, w2_all, layer_idx):
    m, d = x.shape
    n_tiles = m // DENSE_TILE
    meta = (jnp.full((n_tiles,), layer_idx, I32), jnp.full((n_tiles,), DENSE_TILE // FFN_SUB, I32),
            jnp.arange(n_tiles, dtype=I32), jnp.full((1,), n_tiles, I32))
    return _swiglu(meta, x, w1_all, w3_all, w2_all, g, DENSE_TILE)


def kernel(x, positions, attn_norm, w_in, q_norm, k_norm, lambda_q1, lambda_k1, lambda_q2, lambda_k2, subln, pool_w, pool_b, pool_scale, w_out, ffn_norm, dense_w1, dense_w3, dense_w2, router_w, moe_w1, moe_w3, moe_w2):
    batch, seq, d = x.shape
    depth = w_in.shape[0]
    m = batch * seq
    xf = x.reshape(m, d)
    pos = positions.reshape(m, 1)
    inv_freq = 1.0 / (ROPE_THETA ** (jnp.arange(0, HEAD_DIM, 2, dtype=F32) / HEAD_DIM))
    invf = jnp.tile(inv_freq, LANES // (HEAD_DIM // 2))[None, :]
    for l in range(depth):
        qk, v, u = _in_proj(xf, attn_norm[l][None, :], w_in, l)
        gain = jnp.concatenate([jnp.tile(q_norm[l], ATTN_WIDTH // HEAD_DIM),
                                jnp.tile(k_norm[l], ATTN_WIDTH // HEAD_DIM)])[None, :]
        qk = _qk_rope(pos, invf, qk, gain)
        lambda_init = 0.8 - 0.6 * math.exp(-0.3 * l)
        a = _attention(jnp.full((1,), lambda_init, F32), qk, v, lambda_q1[l][None, :], lambda_k1[l][None, :],
                       lambda_q2[l][None, :], lambda_k2[l][None, :], subln[l][:, None], batch, seq)
        p = _pool(u, pool_w, l, pool_b[l].reshape(1, POOL_WIDTH), pool_scale[l][None, :], seq)
        xf = _out_proj(a, p, w_out, l, xf)
        i = l // 2
        if l % 2 == 0:
            xf = _dense(xf, ffn_norm[l][None, :], dense_w1, dense_w3, dense_w2, i)
        else:
            xf = _moe(xf, ffn_norm[l][None, :], router_w[i], moe_w1, moe_w3, moe_w2, i)
    return xf.reshape(batch, seq, d)
```

```python
import functools
import math

import jax
import jax.numpy as jnp
from jax import lax
from jax.experimental import pallas as pl
from jax.experimental.pallas import tpu as pltpu

F32 = jnp.float32
BF16 = jnp.bfloat16
I32 = jnp.int32

D_MODEL = 2048
ATTN_WIDTH = 1024
POOL_WIDTH = 1024
HEAD_DIM = 64
N_HEADS = 8
HEAD_WIDTH = 2 * HEAD_DIM
POOL_WINDOWS = (2, 4, 8, 16)
POOL_GROUP_DIM = 256
ROPE_THETA = 10000.0
NORM_EPS = 1e-6
N_EXPERTS = 8
TOP_K = 2
NEG_BIG = -1e30
LOG2_E = math.log2(math.e)

LANES = 128
VMEM_BUDGET = 56 * 1024 * 1024

PROJ_BM = 2048
PROJ_BN = 512
IN_PROJ_BM = 2048
IN_PROJ_BN = 512
X_CHUNK = 256
NORM_CHUNK = 128
ROPE_BM = 256
ATTN_BQ = 512
ATTN_STREAMS = 2
POOL_BM = 256
FFN_SUB = 256
FFN_MM = 512
FFN_TRIP = 4
FFN_BF = 256
DENSE_TILE = 2048
MOE_TILE = 2560
GATHER_BM = 256
DMA_UNROLL = 8


def _cparams(sem, vmem=None):
    return pltpu.CompilerParams(dimension_semantics=sem, vmem_limit_bytes=vmem)


def _split_bf16(x):
    hi = x.astype(BF16)
    lo = (x - hi.astype(F32)).astype(BF16)
    return hi, lo


def _rms_rows(x_ref, g_ref, h_ref, rows):
    def body(c, carry):
        r = pl.multiple_of(c * NORM_CHUNK, NORM_CHUNK)
        x = x_ref[pl.ds(r, NORM_CHUNK), :]
        ms = jnp.mean(x * x, axis=-1, keepdims=True)
        h_ref[pl.ds(r, NORM_CHUNK), :] = (x * lax.rsqrt(ms + NORM_EPS) * g_ref[...]).astype(h_ref.dtype)
        return carry
    lax.fori_loop(0, rows // NORM_CHUNK, body, 0)


def _in_proj_kernel(x_hbm, g_ref, w_ref, cos_ref, sin_ref, gain_ref, qk_ref, v_ref, u_ref, h_ref, xbuf, sem,
                    *, n_qk, n_v):
    i = pl.program_id(0)
    j = pl.program_id(1)
    bm = h_ref.shape[0]
    chunk = xbuf.shape[1]
    n_chunks = bm // chunk

    @pl.when(j == 0)
    def _():
        def chunk_copy(c, slot):
            rows = pl.ds(pl.multiple_of(i * bm + c * chunk, chunk), chunk)
            return pltpu.make_async_copy(x_hbm.at[rows], xbuf.at[slot], sem.at[slot])

        chunk_copy(0, 0).start()
        for c in range(n_chunks):
            slot = c % 2
            if c + 1 < n_chunks:
                chunk_copy(c + 1, 1 - slot).start()
            chunk_copy(c, slot).wait()
            _rms_rows(xbuf.at[slot], g_ref, h_ref.at[pl.ds(c * chunk, chunk)], chunk)

    def proj():
        return jnp.dot(h_ref[...], w_ref[...].astype(BF16), preferred_element_type=F32)

    @pl.when(j < n_qk)
    def _():
        y = proj()
        lane = lax.broadcasted_iota(I32, (1, LANES), 1)
        first_half = (lane % HEAD_DIM) < (HEAD_DIM // 2)
        gr = lax.broadcasted_iota(I32, (LANES, LANES), 0) // HEAD_DIM
        gc = lax.broadcasted_iota(I32, (LANES, LANES), 1) // HEAD_DIM
        group_ones = jnp.where(gr == gc, 1.0, 0.0).astype(BF16)
        scale = jnp.where(j < n_qk // 2, HEAD_DIM ** -0.5 * LOG2_E, 1.0)
        cos = cos_ref[...]
        sin_signed = sin_ref[...]
        for hb in range(y.shape[1] // LANES):
            cols = slice(hb * LANES, (hb + 1) * LANES)
            x = y[:, cols]
            ssum = jnp.dot((x * x).astype(BF16), group_ones, preferred_element_type=F32)
            z = x * lax.rsqrt(ssum * (1.0 / HEAD_DIM) + NORM_EPS) * gain_ref[:, cols]
            swapped = jnp.where(first_half, pltpu.roll(z, LANES - HEAD_DIM // 2, 1),
                                pltpu.roll(z, HEAD_DIM // 2, 1))
            qk_ref[:, cols] = ((z * cos + swapped * sin_signed) * scale).astype(BF16)

    @pl.when((j >= n_qk) & (j < n_qk + n_v))
    def _():
        v_ref[...] = proj().astype(BF16)

    @pl.when(j >= n_qk + n_v)
    def _():
        u_ref[...] = proj()


def _in_proj(x, g, w_all, layer, cos, sin_signed, qk_gain):
    m, d = x.shape
    bn = IN_PROJ_BN
    bm = IN_PROJ_BM
    n_qk = 2 * ATTN_WIDTH // bn
    n_v = ATTN_WIDTH // bn
    n_u = POOL_WIDTH // bn
    assert w_all.shape[2] == (n_qk + n_v + n_u) * bn and m % bm == 0
    return pl.pallas_call(
        functools.partial(_in_proj_kernel, n_qk=n_qk, n_v=n_v),
        grid=(m // bm, n_qk + n_v + n_u),
        in_specs=[
            pl.BlockSpec(memory_space=pl.ANY),
            pl.BlockSpec((1, d), lambda i, j: (0, 0)),
            pl.BlockSpec((None, d, bn), lambda i, j: (layer, 0, j)),
            pl.BlockSpec((bm, LANES), lambda i, j: (i, 0)),
            pl.BlockSpec((bm, LANES), lambda i, j: (i, 0)),
            pl.BlockSpec((1, bn), lambda i, j: (0, jnp.minimum(j, n_qk - 1))),
        ],
        out_specs=[
            pl.BlockSpec((bm, bn), lambda i, j: (i, jnp.minimum(j, n_qk - 1))),
            pl.BlockSpec((bm, bn), lambda i, j: (i, jnp.clip(j - n_qk, 0, n_v - 1))),
            pl.BlockSpec((bm, bn), lambda i, j: (i, jnp.clip(j - n_qk - n_v, 0, n_u - 1))),
        ],
        out_shape=[
            jax.ShapeDtypeStruct((m, 2 * ATTN_WIDTH), BF16),
            jax.ShapeDtypeStruct((m, ATTN_WIDTH), BF16),
            jax.ShapeDtypeStruct((m, POOL_WIDTH), F32),
        ],
        scratch_shapes=[
            pltpu.VMEM((bm, d), BF16),
            pltpu.VMEM((2, X_CHUNK, d), F32),
            pltpu.SemaphoreType.DMA((2,)),
        ],
        compiler_params=_cparams(("arbitrary", "arbitrary"), VMEM_BUDGET),
        name="in_proj",
    )(x, g, w_all, cos, sin_signed, qk_gain)


def _rope_tables_kernel(pos_ref, invf_ref, cos_ref, sin_ref):
    ang = pos_ref[...].astype(F32) * invf_ref[...]
    lane = lax.broadcasted_iota(I32, ang.shape, 1)
    first_half = (lane % HEAD_DIM) < (HEAD_DIM // 2)
    sin = jnp.sin(ang)
    cos_ref[...] = jnp.cos(ang)
    sin_ref[...] = jnp.where(first_half, -sin, sin)


def _rope_tables(pos, invf):
    m = pos.shape[0]
    table = jax.ShapeDtypeStruct((m, LANES), F32)
    return pl.pallas_call(
        _rope_tables_kernel,
        grid=(m // ROPE_BM,),
        in_specs=[
            pl.BlockSpec((ROPE_BM, 1), lambda i: (i, 0)),
            pl.BlockSpec((1, LANES), lambda i: (0, 0)),
        ],
        out_specs=[pl.BlockSpec((ROPE_BM, LANES), lambda i: (i, 0))] * 2,
        out_shape=[table, table],
        compiler_params=_cparams(("arbitrary",)),
        name="rope_tables",
    )(pos, invf)


def _block_streams(nq):
    chains = [[(qi, kb) for kb in range(qi + 1)] for qi in range(nq - 1, -1, -1)]
    streams = tuple([] for _ in range(ATTN_STREAMS))
    for chain in chains:
        min(streams, key=len).extend(chain)
    order = []
    for i in range(max(len(s) for s in streams)):
        order.extend(s[i] for s in streams if i < len(s))
    return order


def _attn_kernel(linit_ref, q_ref, k_ref, v_ref, lq1_ref, lk1_ref, lq2_ref, lk2_ref, sg_ref, o_ref,
                 q2_ref, vt_ref, m_ref, l_ref, acc_ref):
    nq, _, bq = vt_ref.shape
    lane = lax.broadcasted_iota(I32, (bq, HEAD_WIDTH), 1)
    for c in range(nq):
        rows = slice(c * bq, (c + 1) * bq)
        vt_ref[c] = v_ref[rows, :].astype(F32).T.astype(BF16)
        q = q_ref[rows, :]
        zero = jnp.zeros_like(q)
        q2_ref[c, 0:bq, :] = jnp.where(lane < HEAD_DIM, q, zero)
        q2_ref[c, bq:2 * bq, :] = jnp.where(lane >= HEAD_DIM, q, zero)
    m_ref[...] = jnp.full(m_ref.shape, NEG_BIG, F32)
    l_ref[...] = jnp.zeros(l_ref.shape, F32)
    acc_ref[...] = jnp.zeros(acc_ref.shape, F32)

    for qi, kb in _block_streams(nq):
        k = k_ref[kb * bq:(kb + 1) * bq, :]
        s = lax.dot_general(k, q2_ref[qi], (((1,), (1,)), ((), ())), preferred_element_type=F32)
        if kb == qi:
            key = lax.broadcasted_iota(I32, s.shape, 0)
            qry = lax.broadcasted_iota(I32, s.shape, 1) & (bq - 1)
            s = jnp.where(key <= qry, s, NEG_BIG)
        m_old = m_ref[qi]
        m_new = jnp.maximum(m_old, jnp.max(s, axis=0, keepdims=True))
        alpha = jnp.exp2(m_old - m_new)
        p = jnp.exp2(s - m_new)
        l_ref[qi] = alpha * l_ref[qi] + jnp.sum(p, axis=0, keepdims=True)
        acc_ref[qi] = alpha * acc_ref[qi] + jnp.dot(vt_ref[kb], p.astype(BF16), preferred_element_type=F32)
        m_ref[qi] = m_new

    lambda_init = linit_ref[0]
    lam = (jnp.exp(jnp.sum(lq1_ref[...] * lk1_ref[...], axis=-1, keepdims=True))
           - jnp.exp(jnp.sum(lq2_ref[...] * lk2_ref[...], axis=-1, keepdims=True))
           + lambda_init)
    for qi in range(nq):
        acc = acc_ref[qi]
        l = l_ref[qi]
        o = acc[:, :bq] / l[:, :bq] - lam * (acc[:, bq:] / l[:, bq:])
        ms = jnp.mean(o * o, axis=0, keepdims=True)
        o = o * lax.rsqrt(ms + NORM_EPS) * sg_ref[...] * (1.0 - lambda_init)
        o_ref[qi * bq:(qi + 1) * bq, :] = o.T.astype(BF16)


def _attention(linit, qk, v, lq1, lk1, lq2, lk2, sg_col, batch, seq):
    m = qk.shape[0]
    nq = seq // ATTN_BQ
    vec = pl.BlockSpec((1, HEAD_DIM), lambda b, h: (0, 0))
    return pl.pallas_call(
        _attn_kernel,
        grid=(batch, N_HEADS),
        in_specs=[
            pl.BlockSpec(memory_space=pltpu.SMEM),
            pl.BlockSpec((seq, HEAD_WIDTH), lambda b, h: (b, h)),
            pl.BlockSpec((seq, HEAD_WIDTH), lambda b, h: (b, N_HEADS + h)),
            pl.BlockSpec((seq, HEAD_WIDTH), lambda b, h: (b, h)),
            vec, vec, vec, vec,
            pl.BlockSpec((HEAD_WIDTH, 1), lambda b, h: (0, 0)),
        ],
        out_specs=pl.BlockSpec((seq, HEAD_WIDTH), lambda b, h: (b, h)),
        out_shape=jax.ShapeDtypeStruct((m, ATTN_WIDTH), BF16),
        scratch_shapes=[
            pltpu.VMEM((nq, 2 * ATTN_BQ, HEAD_WIDTH), BF16),
            pltpu.VMEM((nq, HEAD_WIDTH, ATTN_BQ), BF16),
            pltpu.VMEM((nq, 1, 2 * ATTN_BQ), F32),
            pltpu.VMEM((nq, 1, 2 * ATTN_BQ), F32),
            pltpu.VMEM((nq, HEAD_WIDTH, 2 * ATTN_BQ), F32),
        ],
        compiler_params=_cparams(("arbitrary", "arbitrary")),
        name="diff_attn",
    )(linit, qk, qk, v, lq1, lk1, lq2, lk2, sg_col)


def _pool_kernel(uc_ref, up_ref, w_ref, b_ref, sc_ref, o_ref, *, chunks_per_seq):
    c = pl.program_id(0) % chunks_per_seq
    bm = uc_ref.shape[0]
    row = lax.broadcasted_iota(I32, (bm, bm), 0)
    col = lax.broadcasted_iota(I32, (bm, bm), 1)
    t = c * bm + lax.broadcasted_iota(I32, (bm, 1), 0)
    has_prev = c > 0
    for g, win in enumerate(POOL_WINDOWS):
        cols = slice(g * POOL_GROUP_DIM, (g + 1) * POOL_GROUP_DIM)
        cur = jnp.where((row >= col) & (row - col < win), 1.0, 0.0).astype(BF16)
        prv = jnp.where(col - row > bm - win, 1.0, 0.0).astype(BF16)
        u = uc_ref[:, cols]
        u_hi, u_lo = _split_bf16(u)
        p_hi, p_lo = _split_bf16(up_ref[:, cols])
        wsum = (jnp.dot(cur, u_hi, preferred_element_type=F32)
                + jnp.dot(cur, u_lo, preferred_element_type=F32))
        wprev = (jnp.dot(prv, p_hi, preferred_element_type=F32)
                 + jnp.dot(prv, p_lo, preferred_element_type=F32))
        wsum = wsum + jnp.where(has_prev, wprev, 0.0)
        cnt = jnp.minimum(t + 1, win).astype(F32)
        d = wsum / cnt - u
        y = jnp.dot(d.astype(BF16), w_ref[g].astype(BF16), preferred_element_type=F32) + b_ref[:, cols]
        o_ref[:, cols] = (y * sc_ref[:, cols]).astype(BF16)


def _pool(u, w_all, layer, b, sc, seq):
    m, width = u.shape
    cps = seq // POOL_BM
    return pl.pallas_call(
        functools.partial(_pool_kernel, chunks_per_seq=cps),
        grid=(m // POOL_BM,),
        in_specs=[
            pl.BlockSpec((POOL_BM, width), lambda i: (i, 0)),
            pl.BlockSpec((POOL_BM, width), lambda i: (jnp.maximum(i - 1, 0), 0)),
            pl.BlockSpec((None,) + w_all.shape[1:], lambda i: (layer, 0, 0, 0)),
            pl.BlockSpec((1, width), lambda i: (0, 0)),
            pl.BlockSpec((1, width), lambda i: (0, 0)),
        ],
        out_specs=pl.BlockSpec((POOL_BM, width), lambda i: (i, 0)),
        out_shape=jax.ShapeDtypeStruct((m, width), BF16),
        compiler_params=_cparams(("arbitrary",)),
        name="pool_mixer",
    )(u, u, w_all, b, sc)


def _out_proj_kernel(a_ref, p_ref, wa_ref, wp_ref, x_ref, o_ref):
    y = jnp.dot(a_ref[...], wa_ref[...].astype(BF16), preferred_element_type=F32)
    y = y + jnp.dot(p_ref[...], wp_ref[...].astype(BF16), preferred_element_type=F32)
    o_ref[...] = x_ref[...] + y


def _out_proj(a, p, w_all, layer, x):
    m, d = x.shape
    ka = a.shape[1]
    kp = p.shape[1]
    assert ka == kp
    return pl.pallas_call(
        _out_proj_kernel,
        grid=(m // PROJ_BM, d // PROJ_BN),
        in_specs=[
            pl.BlockSpec((PROJ_BM, ka), lambda i, j: (i, 0)),
            pl.BlockSpec((PROJ_BM, kp), lambda i, j: (i, 0)),
            pl.BlockSpec((None, ka, PROJ_BN), lambda i, j: (layer, 0, j)),
            pl.BlockSpec((None, kp, PROJ_BN), lambda i, j: (layer, 1, j)),
            pl.BlockSpec((PROJ_BM, PROJ_BN), lambda i, j: (i, j)),
        ],
        out_specs=pl.BlockSpec((PROJ_BM, PROJ_BN), lambda i, j: (i, j)),
        out_shape=jax.ShapeDtypeStruct((m, d), F32),
        compiler_params=_cparams(("arbitrary", "arbitrary"), VMEM_BUDGET),
        name="out_proj",
    )(a, p, w_all, w_all, x)


def _route_kernel(x_ref, g_ref, rw_ref, h_ref, idx_ref, gate_ref):
    x = x_ref[...]
    ms = jnp.mean(x * x, axis=-1, keepdims=True)
    h = x * lax.rsqrt(ms + NORM_EPS) * g_ref[...]
    h_hi, h_lo = _split_bf16(h)
    half = h.shape[1] // 2
    bits = lax.bitcast_convert_type(h_hi.astype(F32), jnp.uint32)
    h_ref[...] = (bits[:, half:] & jnp.uint32(0xFFFF0000)) | (bits[:, :half] >> 16)
    w_hi, w_lo = _split_bf16(rw_ref[...])
    logits = (jnp.dot(h_hi, w_hi, preferred_element_type=F32)
              + jnp.dot(h_hi, w_lo, preferred_element_type=F32)
              + jnp.dot(h_lo, w_hi, preferred_element_type=F32))
    lane = lax.broadcasted_iota(I32, logits.shape, 1)
    logits = jnp.where(lane < N_EXPERTS, logits, NEG_BIG)
    v1 = jnp.max(logits, axis=-1, keepdims=True)
    i1 = jnp.min(jnp.where(logits == v1, lane, LANES), axis=-1, keepdims=True)
    rest = jnp.where(lane == i1, NEG_BIG, logits)
    v2 = jnp.max(rest, axis=-1, keepdims=True)
    i2 = jnp.min(jnp.where(rest == v2, lane, LANES), axis=-1, keepdims=True)
    e = jnp.exp(v2 - v1)
    g1 = 1.0 / (1.0 + e)
    g2 = e / (1.0 + e)
    idx_ref[...] = jnp.where(lane == 0, i1, jnp.where(lane == 1, i2, 0))
    gate_ref[...] = jnp.where(lane == 0, g1, jnp.where(lane == 1, g2, 0.0))


def _route(x, g, rw_padded):
    m, d = x.shape
    bm = 256
    return pl.pallas_call(
        _route_kernel,
        grid=(m // bm,),
        in_specs=[
            pl.BlockSpec((bm, d), lambda i: (i, 0)),
            pl.BlockSpec((1, d), lambda i: (0, 0)),
            pl.BlockSpec((d, LANES), lambda i: (0, 0)),
        ],
        out_specs=[
            pl.BlockSpec((bm, d // 2), lambda i: (i, 0)),
            pl.BlockSpec((bm, LANES), lambda i: (i, 0)),
            pl.BlockSpec((bm, LANES), lambda i: (i, 0)),
        ],
        out_shape=[
            jax.ShapeDtypeStruct((m, d // 2), jnp.uint32),
            jax.ShapeDtypeStruct((m, LANES), I32),
            jax.ShapeDtypeStruct((m, LANES), F32),
        ],
        compiler_params=_cparams(("arbitrary",)),
        name="ffn_norm_route",
    )(x, g, rw_padded)


def _swiglu_kernel(eid_ref, nsub_ref, blk_ref, tot_ref, x_ref, w1_ref, w3_ref, w2_ref, *rest, has_res):
    if has_res:
        gain_ref, o_ref, w1s, w3s, w2s, h_ref, sem = rest
    else:
        o_ref, w1s, w3s, w2s = rest
        h_ref = x_ref
    t = pl.program_id(0)
    j = pl.program_id(1)
    nsub = nsub_ref[t]
    tile_rows = o_ref.shape[0]

    @pl.when(j == 0)
    def _():
        if has_res:
            seed = pltpu.make_async_copy(
                x_ref.at[pl.ds(pl.multiple_of(t * tile_rows, tile_rows), tile_rows)], o_ref, sem)
            seed.start()
            seed.wait()
            _rms_rows(o_ref, gain_ref, h_ref, tile_rows)
        else:
            def init(i, carry):
                rows = pl.ds(pl.multiple_of(i * FFN_SUB, FFN_SUB), FFN_SUB)
                o_ref[rows, :] = jnp.zeros((FFN_SUB, o_ref.shape[1]), F32)
                return carry
            lax.fori_loop(0, tile_rows // FFN_SUB, init, 0)

    def row_block(start, n_rows):
        rows = pl.ds(pl.multiple_of(start, FFN_SUB), n_rows)
        xs = h_ref[rows, :]
        a = jnp.dot(xs, w1s[...], preferred_element_type=F32)
        b = jnp.dot(xs, w3s[...], preferred_element_type=F32)
        hidden = (a * jax.nn.sigmoid(a) * b).astype(BF16)
        o_ref[rows, :] += jnp.dot(hidden, w2s[...], preferred_element_type=F32)

    def cast_weights():
        w1s[...] = w1_ref[0].astype(BF16)
        w3s[...] = w3_ref[0].astype(BF16)
        w2s[...] = w2_ref[0].astype(BF16)

    if has_res:
        cast_weights()
        for i in range(o_ref.shape[0] // FFN_MM):
            row_block(i * FFN_MM, FFN_MM)
    else:
        @pl.when(nsub > 0)
        def _():
            cast_weights()
            n_mm = nsub // (FFN_MM // FFN_SUB)
            n_trips = n_mm // FFN_TRIP

            def trip(i, carry):
                for k in range(FFN_TRIP):
                    row_block((i * FFN_TRIP + k) * FFN_MM, FFN_MM)
                return carry
            lax.fori_loop(0, n_trips, trip, 0)

            def single(i, carry):
                row_block(i * FFN_MM, FFN_MM)
                return carry
            lax.fori_loop(n_trips * FFN_TRIP, n_mm, single, 0)

            @pl.when(nsub % (FFN_MM // FFN_SUB) == 1)
            def _():
                row_block((nsub - 1) * FFN_SUB, FFN_SUB)


def _swiglu(meta, x, w1, w3, w2, gain, tile_rows):
    eid, nsub, blk, tot = meta
    n_tiles = eid.shape[0]
    d = x.shape[1]
    f = w1.shape[2]
    n_j = f // FFN_BF
    has_res = gain is not None

    def row_map(t, j, eid, nsub, blk, tot):
        return (blk[t], 0)

    def out_map(t, j, eid, nsub, blk, tot):
        return (t, 0)

    def hidden_block(t, j, tot):
        return jnp.where(t < tot[0], j, n_j - 1)

    def w13_map(t, j, eid, nsub, blk, tot):
        return (eid[t], 0, hidden_block(t, j, tot))

    def w2_map(t, j, eid, nsub, blk, tot):
        return (eid[t], hidden_block(t, j, tot), 0)

    once = pl.Buffered(1)
    in_specs = [
        pl.BlockSpec(memory_space=pl.ANY) if has_res else pl.BlockSpec((tile_rows, d), row_map, pipeline_mode=once),
        pl.BlockSpec((1, d, FFN_BF), w13_map),
        pl.BlockSpec((1, d, FFN_BF), w13_map),
        pl.BlockSpec((1, FFN_BF, d), w2_map),
    ]
    args = [x, w1, w3, w2]
    scratch = [
        pltpu.VMEM((d, FFN_BF), BF16),
        pltpu.VMEM((d, FFN_BF), BF16),
        pltpu.VMEM((FFN_BF, d), BF16),
    ]
    if has_res:
        in_specs.append(pl.BlockSpec((1, d), lambda t, j, *_: (0, 0)))
        args.append(gain)
        scratch.append(pltpu.VMEM((tile_rows, d), BF16))
        scratch.append(pltpu.SemaphoreType.DMA(()))
    return pl.pallas_call(
        functools.partial(_swiglu_kernel, has_res=has_res),
        grid_spec=pltpu.PrefetchScalarGridSpec(
            num_scalar_prefetch=4,
            grid=(n_tiles, n_j),
            in_specs=in_specs,
            out_specs=pl.BlockSpec((tile_rows, d), out_map, pipeline_mode=once),
            scratch_shapes=scratch,
        ),
        out_shape=jax.ShapeDtypeStruct((n_tiles * tile_rows, d), F32),
        compiler_params=_cparams(("arbitrary", "arbitrary"), VMEM_BUDGET),
        name="swiglu_res" if has_res else "swiglu_moe",
    )(eid, nsub, blk, tot, *args)


def _gather_kernel(nvalid_ref, tok_ref, tok_next_ref, h_hbm, o_ref, buf, sem):
    i = pl.program_id(0)
    nb = pl.num_programs(0)
    n = o_ref.shape[0]
    slot = i % 2
    nxt = jnp.minimum(i + 1, nb - 1)

    def row_copy(idx_ref, r, s):
        return pltpu.make_async_copy(h_hbm.at[pl.ds(idx_ref[0, 0, r], 1)], buf.at[s, pl.ds(r, 1)], sem.at[s])

    def start_all(idx_ref, s):
        def start(g, c):
            for k in range(DMA_UNROLL):
                row_copy(idx_ref, g * DMA_UNROLL + k, s).start(priority=k % 2)
            return c
        lax.fori_loop(0, n // DMA_UNROLL, start, 0)

    @pl.when((i == 0) & (nvalid_ref[0] > 0))
    def _():
        start_all(tok_ref, 0)

    @pl.when((i + 1 < nb) & (nvalid_ref[nxt] > 0))
    def _():
        start_all(tok_next_ref, 1 - slot)

    valid = nvalid_ref[i] > 0

    @pl.when(valid)
    def _():
        def wait(r, c):
            row_copy(tok_ref, r, slot).wait()
            return c
        lax.fori_loop(0, n, wait, 0, unroll=DMA_UNROLL)
        words = buf[slot]
        half = words.shape[1]
        o_ref[:, :half] = lax.bitcast_convert_type(words << 16, F32).astype(BF16)
        o_ref[:, half:] = lax.bitcast_convert_type(words & jnp.uint32(0xFFFF0000), F32).astype(BF16)

    @pl.when(jnp.logical_not(valid))
    def _():
        o_ref[...] = jnp.zeros(o_ref.shape, BF16)


def _gather_rows(nvalid_blocks, tok_of_row, h):
    n_rows = tok_of_row.shape[0]
    d = h.shape[1]
    nb = n_rows // GATHER_BM
    tok3 = tok_of_row.reshape(nb, 1, GATHER_BM)
    idx_block = (1, 1, GATHER_BM)
    return pl.pallas_call(
        _gather_kernel,
        grid_spec=pltpu.PrefetchScalarGridSpec(
            num_scalar_prefetch=1,
            grid=(nb,),
            in_specs=[
                pl.BlockSpec(idx_block, lambda i, nv: (i, 0, 0), memory_space=pltpu.SMEM),
                pl.BlockSpec(idx_block, lambda i, nv: (jnp.minimum(i + 1, nb - 1), 0, 0), memory_space=pltpu.SMEM),
                pl.BlockSpec(memory_space=pl.ANY),
            ],
            out_specs=pl.BlockSpec((GATHER_BM, 2 * d), lambda i, nv: (i, 0)),
            scratch_shapes=[pltpu.VMEM((2, GATHER_BM, d), jnp.uint32), pltpu.SemaphoreType.DMA((2,))],
        ),
        out_shape=jax.ShapeDtypeStruct((n_rows, 2 * d), BF16),
        compiler_params=_cparams(("arbitrary",)),
        name="moe_gather",
    )(nvalid_blocks, tok3, tok3, h)


def _combine_kernel(pos_ref, pos_next_ref, x_ref, gate_ref, y_hbm, o_ref, buf, sem):
    i = pl.program_id(0)
    nb = pl.num_programs(0)
    n = o_ref.shape[0]
    slot = i % 2

    def row_copy(idx_ref, r, k, s):
        return pltpu.make_async_copy(y_hbm.at[pl.ds(idx_ref[0, k, r], 1)], buf.at[s, k, pl.ds(r, 1)], sem.at[s])

    def start_all(idx_ref, s):
        def start(g, c):
            for k in range(DMA_UNROLL):
                r = g * DMA_UNROLL + k
                row_copy(idx_ref, r, 0, s).start(priority=0)
                row_copy(idx_ref, r, 1, s).start(priority=1)
            return c
        lax.fori_loop(0, n // DMA_UNROLL, start, 0)

    @pl.when(i == 0)
    def _():
        start_all(pos_ref, 0)

    @pl.when(i + 1 < nb)
    def _():
        start_all(pos_next_ref, 1 - slot)

    def wait(r, c):
        row_copy(pos_ref, r, 0, slot).wait()
        row_copy(pos_ref, r, 1, slot).wait()
        return c
    lax.fori_loop(0, n, wait, 0, unroll=DMA_UNROLL)
    g = gate_ref[...]
    o_ref[...] = x_ref[...] + g[:, 0:1] * buf[slot, 0] + g[:, 1:2] * buf[slot, 1]


def _combine(pos, x, gate, y):
    m, d = x.shape
    nb = m // GATHER_BM
    pos3 = pos.reshape(nb, GATHER_BM, TOP_K).transpose(0, 2, 1)
    idx_block = (1, TOP_K, GATHER_BM)
    return pl.pallas_call(
        _combine_kernel,
        grid=(nb,),
        in_specs=[
            pl.BlockSpec(idx_block, lambda i: (i, 0, 0), memory_space=pltpu.SMEM),
            pl.BlockSpec(idx_block, lambda i: (jnp.minimum(i + 1, nb - 1), 0, 0), memory_space=pltpu.SMEM),
            pl.BlockSpec((GATHER_BM, d), lambda i: (i, 0)),
            pl.BlockSpec((GATHER_BM, LANES), lambda i: (i, 0)),
            pl.BlockSpec(memory_space=pl.ANY),
        ],
        out_specs=pl.BlockSpec((GATHER_BM, d), lambda i: (i, 0)),
        out_shape=jax.ShapeDtypeStruct((m, d), F32),
        scratch_shapes=[pltpu.VMEM((2, TOP_K, GATHER_BM, d), F32), pltpu.SemaphoreType.DMA((2,))],
        compiler_params=_cparams(("arbitrary",)),
        name="moe_combine",
    )(pos3, pos3, x, gate, y)


def _dispatch_plan(idx, n_tokens):
    n_assign = n_tokens * TOP_K
    n_tiles = n_assign // MOE_TILE + N_EXPERTS
    e_flat = idx.reshape(n_assign)
    onehot = (e_flat[:, None] == jnp.arange(N_EXPERTS, dtype=I32)[None, :]).astype(I32)
    counts = jnp.sum(onehot, axis=0)
    rank = jnp.take_along_axis(jnp.cumsum(onehot, axis=0) - onehot, e_flat[:, None], axis=1)[:, 0]
    tiles_e = (counts + MOE_TILE - 1) // MOE_TILE
    tile_end = jnp.cumsum(tiles_e)
    tile_start = tile_end - tiles_e
    total = tile_end[-1]
    pos = tile_start[e_flat] * MOE_TILE + rank
    t = jnp.arange(n_tiles, dtype=I32)
    t_eff = jnp.minimum(t, total - 1)
    eid = jnp.minimum(jnp.sum((t_eff[:, None] >= tile_end[None, :]).astype(I32), axis=1), N_EXPERTS - 1)
    rows_valid = jnp.clip(counts[eid] - (t_eff - tile_start[eid]) * MOE_TILE, 0, MOE_TILE)
    rows_valid = jnp.where(t < total, rows_valid, 0)
    nsub = (rows_valid + FFN_SUB - 1) // FFN_SUB
    sub_per_tile = MOE_TILE // GATHER_BM
    sub_id = jnp.arange(n_tiles * sub_per_tile, dtype=I32)
    nvalid_blocks = ((sub_id % sub_per_tile) < nsub[sub_id // sub_per_tile]).astype(I32)
    tok_of_row = jnp.zeros((n_tiles * MOE_TILE,), I32).at[pos].set(jnp.arange(n_assign, dtype=I32) // TOP_K)
    meta = (eid, nsub.astype(I32), t_eff, jnp.reshape(total, (1,)).astype(I32))
    return meta, nvalid_blocks, tok_of_row, pos.reshape(n_tokens, TOP_K)


def _moe(x, g, rw, w1_all, w3_all, w2_all, layer_idx):
    m, d = x.shape
    w1 = w1_all.reshape((-1,) + w1_all.shape[2:])
    w3 = w3_all.reshape((-1,) + w3_all.shape[2:])
    w2 = w2_all.reshape((-1,) + w2_all.shape[2:])
    rw_padded = jnp.zeros((d, LANES), F32).at[:, :N_EXPERTS].set(rw)
    h, idx, gate = _route(x, g, rw_padded)
    meta, nvalid_blocks, tok_of_row, pos = _dispatch_plan(idx[:, :TOP_K], m)
    meta = (meta[0] + layer_idx * N_EXPERTS,) + meta[1:]
    xs = _gather_rows(nvalid_blocks, tok_of_row, h)
    y = _swiglu(meta, xs, w1, w3, w2, None, MOE_TILE)
    return _combine(pos, x, gate, y)


def _dense(x, g, w1_all, w3_all, w2_all, layer_idx):
    m, d = x.shape
    n_tiles = m // DENSE_TILE
    meta = (jnp.full((n_tiles,), layer_idx, I32), jnp.full((n_tiles,), DENSE_TILE // FFN_SUB, I32),
            jnp.arange(n_tiles, dtype=I32), jnp.full((1,), n_tiles, I32))
    return _swiglu(meta, x, w1_all, w3_all, w2_all, g, DENSE_TILE)


def kernel(x, positions, attn_norm, w_in, q_norm, k_norm, lambda_q1, lambda_k1, lambda_q2, lambda_k2, subln, pool_w, pool_b, pool_scale, w_out, ffn_norm, dense_w1, dense_w3, dense_w2, router_w, moe_w1, moe_w3, moe_w2):
    batch, seq, d = x.shape
    depth = w_in.shape[0]
    m = batch * seq
    xf = x.reshape(m, d)
    pos = positions.reshape(m, 1)
    inv_freq = 1.0 / (ROPE_THETA ** (jnp.arange(0, HEAD_DIM, 2, dtype=F32) / HEAD_DIM))
    invf = jnp.tile(inv_freq, LANES // (HEAD_DIM // 2))[None, :]
    cos, sin_signed = _rope_tables(pos, invf)
    for l in range(depth):
        gain = jnp.concatenate([jnp.tile(q_norm[l], ATTN_WIDTH // HEAD_DIM),
                                jnp.tile(k_norm[l], ATTN_WIDTH // HEAD_DIM)])[None, :]
        qk, v, u = _in_proj(xf, attn_norm[l][None, :], w_in, l, cos, sin_signed, gain)
        lambda_init = 0.8 - 0.6 * math.exp(-0.3 * l)
        a = _attention(jnp.full((1,), lambda_init, F32), qk, v, lambda_q1[l][None, :], lambda_k1[l][None, :],
                       lambda_q2[l][None, :], lambda_k2[l][None, :], subln[l][:, None], batch, seq)
        p = _pool(u, pool_w, l, pool_b[l].reshape(1, POOL_WIDTH), pool_scale[l][None, :], seq)
        xf = _out_proj(a, p, w_out, l, xf)
        i = l // 2
        if l % 2 == 0:
            xf = _dense(xf, ffn_norm[l][None, :], dense_w1, dense_w3, dense_w2, i)
        else:
            xf = _moe(xf, ffn_norm[l][None, :], router_w[i], moe_w1, moe_w3, moe_w2, i)
    return xf.reshape(batch, seq, d)
```

```python
import functools
import math

import jax
import jax.numpy as jnp
from jax import lax
from jax.experimental import pallas as pl
from jax.experimental.pallas import tpu as pltpu

F32 = jnp.float32
BF16 = jnp.bfloat16
I32 = jnp.int32

D_MODEL = 2048
ATTN_WIDTH = 1024
POOL_WIDTH = 1024
HEAD_DIM = 64
N_HEADS = 8
HEAD_WIDTH = 2 * HEAD_DIM
POOL_WINDOWS = (2, 4, 8, 16)
POOL_GROUP_DIM = 256
ROPE_THETA = 10000.0
NORM_EPS = 1e-6
N_EXPERTS = 8
TOP_K = 2
NEG_BIG = -1e30
LOG2_E = math.log2(math.e)

LANES = 128
VMEM_BUDGET = 56 * 1024 * 1024

PROJ_BM = 2048
PROJ_BN = 512
IN_PROJ_BM = 2048
IN_PROJ_BN = 512
QK_SLAB = 512
X_CHUNK = 256
NORM_CHUNK = 128
ROPE_BM = 256
ATTN_BQ = 512
ONES_ROWS = 16
ATTN_STREAMS = 2
POOL_BM = 256
FFN_SUB = 256
FFN_MM = 512
FFN_TRIP = 4
FFN_BF = 256
DENSE_TILE = 2048
MOE_TILE = 2560
GATHER_BM = 256
DMA_UNROLL = 8


def _cparams(sem, vmem=None):
    return pltpu.CompilerParams(dimension_semantics=sem, vmem_limit_bytes=vmem)


def _split_bf16(x):
    hi = x.astype(BF16)
    lo = (x - hi.astype(F32)).astype(BF16)
    return hi, lo


def _rms_rows(x_ref, g_ref, h_ref, rows):
    def body(c, carry):
        r = pl.multiple_of(c * NORM_CHUNK, NORM_CHUNK)
        x = x_ref[pl.ds(r, NORM_CHUNK), :]
        ms = jnp.mean(x * x, axis=-1, keepdims=True)
        h_ref[pl.ds(r, NORM_CHUNK), :] = (x * lax.rsqrt(ms + NORM_EPS) * g_ref[...]).astype(h_ref.dtype)
        return carry
    lax.fori_loop(0, rows // NORM_CHUNK, body, 0)


def _in_proj_kernel(x_hbm, g_ref, w_ref, cos_ref, sin_ref, gain_ref, qk_ref, v_ref, u_ref, h_ref, xbuf, sem,
                    *, n_qk, n_v):
    i = pl.program_id(0)
    j = pl.program_id(1)
    bm = h_ref.shape[0]
    chunk = xbuf.shape[1]
    n_chunks = bm // chunk

    @pl.when(j == 0)
    def _():
        def chunk_copy(c, slot):
            rows = pl.ds(pl.multiple_of(i * bm + c * chunk, chunk), chunk)
            return pltpu.make_async_copy(x_hbm.at[rows], xbuf.at[slot], sem.at[slot])

        chunk_copy(0, 0).start()
        for c in range(n_chunks):
            slot = c % 2
            if c + 1 < n_chunks:
                chunk_copy(c + 1, 1 - slot).start()
            chunk_copy(c, slot).wait()
            _rms_rows(xbuf.at[slot], g_ref, h_ref.at[pl.ds(c * chunk, chunk)], chunk)

    def proj():
        return jnp.dot(h_ref[...], w_ref[...].astype(BF16), preferred_element_type=F32)

    def qk_epilogue(y, rows):
        lane = lax.broadcasted_iota(I32, (1, LANES), 1)
        first_half = (lane % HEAD_DIM) < (HEAD_DIM // 2)
        wide = 2 * LANES
        gr = lax.broadcasted_iota(I32, (wide, wide), 0) // HEAD_DIM
        gc = lax.broadcasted_iota(I32, (wide, wide), 1) // HEAD_DIM
        group_ones = jnp.where(gr == gc, 1.0, 0.0).astype(BF16)
        scale = jnp.where(j < n_qk // 2, HEAD_DIM ** -0.5 * LOG2_E, 1.0)
        cos = cos_ref[rows, :]
        sin_signed = sin_ref[rows, :]
        for wb in range(y.shape[1] // wide):
            x2 = y[:, wb * wide:(wb + 1) * wide]
            ssum2 = jnp.dot((x2 * x2).astype(BF16), group_ones, preferred_element_type=F32)
            for hb in range(2):
                cols = slice(wb * wide + hb * LANES, wb * wide + (hb + 1) * LANES)
                x = x2[:, hb * LANES:(hb + 1) * LANES]
                ssum = ssum2[:, hb * LANES:(hb + 1) * LANES]
                z = x * lax.rsqrt(ssum * (1.0 / HEAD_DIM) + NORM_EPS) * gain_ref[:, cols]
                swapped = jnp.where(first_half, pltpu.roll(z, LANES - HEAD_DIM // 2, 1),
                                    pltpu.roll(z, HEAD_DIM // 2, 1))
                qk_ref[rows, cols] = ((z * cos + swapped * sin_signed) * scale).astype(BF16)

    @pl.when(j < n_qk)
    def _():
        w = w_ref[...].astype(BF16)
        for r0 in range(0, bm, QK_SLAB):
            rows = slice(r0, r0 + QK_SLAB)
            qk_epilogue(jnp.dot(h_ref[rows, :], w, preferred_element_type=F32), rows)

    @pl.when((j >= n_qk) & (j < n_qk + n_v))
    def _():
        v_ref[...] = proj().astype(BF16)

    @pl.when(j >= n_qk + n_v)
    def _():
        u_ref[...] = proj()


def _in_proj(x, g, w_all, layer, cos, sin_signed, qk_gain):
    m, d = x.shape
    bn = IN_PROJ_BN
    bm = IN_PROJ_BM
    n_qk = 2 * ATTN_WIDTH // bn
    n_v = ATTN_WIDTH // bn
    n_u = POOL_WIDTH // bn
    assert w_all.shape[2] == (n_qk + n_v + n_u) * bn and m % bm == 0
    return pl.pallas_call(
        functools.partial(_in_proj_kernel, n_qk=n_qk, n_v=n_v),
        grid=(m // bm, n_qk + n_v + n_u),
        in_specs=[
            pl.BlockSpec(memory_space=pl.ANY),
            pl.BlockSpec((1, d), lambda i, j: (0, 0)),
            pl.BlockSpec((None, d, bn), lambda i, j: (layer, 0, j)),
            pl.BlockSpec((bm, LANES), lambda i, j: (i, 0)),
            pl.BlockSpec((bm, LANES), lambda i, j: (i, 0)),
            pl.BlockSpec((1, bn), lambda i, j: (0, jnp.minimum(j, n_qk - 1))),
        ],
        out_specs=[
            pl.BlockSpec((bm, bn), lambda i, j: (i, jnp.minimum(j, n_qk - 1))),
            pl.BlockSpec((bm, bn), lambda i, j: (i, jnp.clip(j - n_qk, 0, n_v - 1))),
            pl.BlockSpec((bm, bn), lambda i, j: (i, jnp.clip(j - n_qk - n_v, 0, n_u - 1))),
        ],
        out_shape=[
            jax.ShapeDtypeStruct((m, 2 * ATTN_WIDTH), BF16),
            jax.ShapeDtypeStruct((m, ATTN_WIDTH), BF16),
            jax.ShapeDtypeStruct((m, POOL_WIDTH), F32),
        ],
        scratch_shapes=[
            pltpu.VMEM((bm, d), BF16),
            pltpu.VMEM((2, X_CHUNK, d), F32),
            pltpu.SemaphoreType.DMA((2,)),
        ],
        compiler_params=_cparams(("arbitrary", "arbitrary"), VMEM_BUDGET),
        name="in_proj",
    )(x, g, w_all, cos, sin_signed, qk_gain)


def _rope_tables_kernel(pos_ref, invf_ref, cos_ref, sin_ref):
    ang = pos_ref[...].astype(F32) * invf_ref[...]
    lane = lax.broadcasted_iota(I32, ang.shape, 1)
    first_half = (lane % HEAD_DIM) < (HEAD_DIM // 2)
    sin = jnp.sin(ang)
    cos_ref[...] = jnp.cos(ang)
    sin_ref[...] = jnp.where(first_half, -sin, sin)


def _rope_tables(pos, invf):
    m = pos.shape[0]
    table = jax.ShapeDtypeStruct((m, LANES), F32)
    return pl.pallas_call(
        _rope_tables_kernel,
        grid=(m // ROPE_BM,),
        in_specs=[
            pl.BlockSpec((ROPE_BM, 1), lambda i: (i, 0)),
            pl.BlockSpec((1, LANES), lambda i: (0, 0)),
        ],
        out_specs=[pl.BlockSpec((ROPE_BM, LANES), lambda i: (i, 0))] * 2,
        out_shape=[table, table],
        compiler_params=_cparams(("arbitrary",)),
        name="rope_tables",
    )(pos, invf)


def _block_streams(nq):
    chains = [[(qi, kb) for kb in range(qi + 1)] for qi in range(nq - 1, -1, -1)]
    streams = tuple([] for _ in range(ATTN_STREAMS))
    for chain in chains:
        min(streams, key=len).extend(chain)
    order = []
    for i in range(max(len(s) for s in streams)):
        order.extend(s[i] for s in streams if i < len(s))
    return order


def _attn_kernel(linit_ref, q_ref, k_ref, v_ref, lq1_ref, lk1_ref, lq2_ref, lk2_ref, sg_ref, o_ref,
                 q2_ref, vt_ref, m_ref, acc_ref):
    nq, _, bq = vt_ref.shape
    lane = lax.broadcasted_iota(I32, (bq, HEAD_WIDTH), 1)
    for c in range(nq):
        rows = slice(c * bq, (c + 1) * bq)
        vt_ref[c, 0:HEAD_WIDTH, :] = v_ref[rows, :].astype(F32).T.astype(BF16)
        vt_ref[c, HEAD_WIDTH:, :] = jnp.ones((ONES_ROWS, bq), BF16)
        q = q_ref[rows, :]
        zero = jnp.zeros_like(q)
        q2_ref[c, 0:bq, :] = jnp.where(lane < HEAD_DIM, q, zero)
        q2_ref[c, bq:2 * bq, :] = jnp.where(lane >= HEAD_DIM, q, zero)
    m_ref[...] = jnp.full(m_ref.shape, NEG_BIG, F32)
    acc_ref[...] = jnp.zeros(acc_ref.shape, F32)

    for qi, kb in _block_streams(nq):
        k = k_ref[kb * bq:(kb + 1) * bq, :]
        s = lax.dot_general(k, q2_ref[qi], (((1,), (1,)), ((), ())), preferred_element_type=F32)
        if kb == qi:
            key = lax.broadcasted_iota(I32, s.shape, 0)
            qry = lax.broadcasted_iota(I32, s.shape, 1) & (bq - 1)
            s = jnp.where(key <= qry, s, NEG_BIG)
        m_old = m_ref[qi]
        m_new = jnp.maximum(m_old, jnp.max(s, axis=0, keepdims=True))
        alpha = jnp.exp2(m_old - m_new)
        p = jnp.exp2(s - m_new)
        acc_ref[qi] = alpha * acc_ref[qi] + jnp.dot(vt_ref[kb], p.astype(BF16), preferred_element_type=F32)
        m_ref[qi] = m_new

    lambda_init = linit_ref[0]
    lam = (jnp.exp(jnp.sum(lq1_ref[...] * lk1_ref[...], axis=-1, keepdims=True))
           - jnp.exp(jnp.sum(lq2_ref[...] * lk2_ref[...], axis=-1, keepdims=True))
           + lambda_init)
    for qi in range(nq):
        acc = acc_ref[qi, 0:HEAD_WIDTH, :]
        l = acc_ref[qi, HEAD_WIDTH:HEAD_WIDTH + 1, :]
        o = acc[:, :bq] / l[:, :bq] - lam * (acc[:, bq:] / l[:, bq:])
        ms = jnp.mean(o * o, axis=0, keepdims=True)
        o = o * lax.rsqrt(ms + NORM_EPS) * sg_ref[...] * (1.0 - lambda_init)
        o_ref[qi * bq:(qi + 1) * bq, :] = o.T.astype(BF16)


def _attention(linit, qk, v, lq1, lk1, lq2, lk2, sg_col, batch, seq):
    m = qk.shape[0]
    nq = seq // ATTN_BQ
    vec = pl.BlockSpec((1, HEAD_DIM), lambda b, h: (0, 0))
    return pl.pallas_call(
        _attn_kernel,
        grid=(batch, N_HEADS),
        in_specs=[
            pl.BlockSpec(memory_space=pltpu.SMEM),
            pl.BlockSpec((seq, HEAD_WIDTH), lambda b, h: (b, h)),
            pl.BlockSpec((seq, HEAD_WIDTH), lambda b, h: (b, N_HEADS + h)),
            pl.BlockSpec((seq, HEAD_WIDTH), lambda b, h: (b, h)),
            vec, vec, vec, vec,
            pl.BlockSpec((HEAD_WIDTH, 1), lambda b, h: (0, 0)),
        ],
        out_specs=pl.BlockSpec((seq, HEAD_WIDTH), lambda b, h: (b, h)),
        out_shape=jax.ShapeDtypeStruct((m, ATTN_WIDTH), BF16),
        scratch_shapes=[
            pltpu.VMEM((nq, 2 * ATTN_BQ, HEAD_WIDTH), BF16),
            pltpu.VMEM((nq, HEAD_WIDTH + ONES_ROWS, ATTN_BQ), BF16),
            pltpu.VMEM((nq, 1, 2 * ATTN_BQ), F32),
            pltpu.VMEM((nq, HEAD_WIDTH + ONES_ROWS, 2 * ATTN_BQ), F32),
        ],
        compiler_params=_cparams(("arbitrary", "arbitrary")),
        name="diff_attn",
    )(linit, qk, qk, v, lq1, lk1, lq2, lk2, sg_col)


def _pool_kernel(uc_ref, up_ref, w_ref, b_ref, sc_ref, o_ref, *, chunks_per_seq):
    c = pl.program_id(0) % chunks_per_seq
    bm = uc_ref.shape[0]
    row = lax.broadcasted_iota(I32, (bm, bm), 0)
    col = lax.broadcasted_iota(I32, (bm, bm), 1)
    t = c * bm + lax.broadcasted_iota(I32, (bm, 1), 0)
    has_prev = c > 0
    halo = max(POOL_WINDOWS)
    hrow = lax.broadcasted_iota(I32, (halo, halo), 0)
    hcol = lax.broadcasted_iota(I32, (halo, halo), 1)
    for g, win in enumerate(POOL_WINDOWS):
        cols = slice(g * POOL_GROUP_DIM, (g + 1) * POOL_GROUP_DIM)
        cur = jnp.where((row >= col) & (row - col < win), 1.0, 0.0).astype(BF16)
        prv = jnp.where(hcol - hrow > halo - win, 1.0, 0.0).astype(BF16)
        u = uc_ref[:, cols]
        u_hi, u_lo = _split_bf16(u)
        p_hi, p_lo = _split_bf16(up_ref[bm - halo:, cols])
        wsum = (jnp.dot(cur, u_hi, preferred_element_type=F32)
                + jnp.dot(cur, u_lo, preferred_element_type=F32))
        wprev = (jnp.dot(prv, p_hi, preferred_element_type=F32)
                 + jnp.dot(prv, p_lo, preferred_element_type=F32))
        wsum = jnp.concatenate([wsum[:halo] + jnp.where(has_prev, wprev, 0.0), wsum[halo:]], axis=0)
        cnt = jnp.minimum(t + 1, win).astype(F32)
        d = wsum / cnt - u
        y = jnp.dot(d.astype(BF16), w_ref[g].astype(BF16), preferred_element_type=F32) + b_ref[:, cols]
        o_ref[:, cols] = (y * sc_ref[:, cols]).astype(BF16)


def _pool(u, w_all, layer, b, sc, seq):
    m, width = u.shape
    cps = seq // POOL_BM
    return pl.pallas_call(
        functools.partial(_pool_kernel, chunks_per_seq=cps),
        grid=(m // POOL_BM,),
        in_specs=[
            pl.BlockSpec((POOL_BM, width), lambda i: (i, 0)),
            pl.BlockSpec((POOL_BM, width), lambda i: (jnp.maximum(i - 1, 0), 0)),
            pl.BlockSpec((None,) + w_all.shape[1:], lambda i: (layer, 0, 0, 0)),
            pl.BlockSpec((1, width), lambda i: (0, 0)),
            pl.BlockSpec((1, width), lambda i: (0, 0)),
        ],
        out_specs=pl.BlockSpec((POOL_BM, width), lambda i: (i, 0)),
        out_shape=jax.ShapeDtypeStruct((m, width), BF16),
        compiler_params=_cparams(("arbitrary",)),
        name="pool_mixer",
    )(u, u, w_all, b, sc)


def _out_proj_kernel(a_ref, p_ref, wa_ref, wp_ref, x_ref, o_ref):
    y = jnp.dot(a_ref[...], wa_ref[...].astype(BF16), preferred_element_type=F32)
    y = y + jnp.dot(p_ref[...], wp_ref[...].astype(BF16), preferred_element_type=F32)
    o_ref[...] = x_ref[...] + y


def _out_proj(a, p, w_all, layer, x):
    m, d = x.shape
    ka = a.shape[1]
    kp = p.shape[1]
    assert ka == kp
    return pl.pallas_call(
        _out_proj_kernel,
        grid=(m // PROJ_BM, d // PROJ_BN),
        in_specs=[
            pl.BlockSpec((PROJ_BM, ka), lambda i, j: (i, 0)),
            pl.BlockSpec((PROJ_BM, kp), lambda i, j: (i, 0)),
            pl.BlockSpec((None, ka, PROJ_BN), lambda i, j: (layer, 0, j)),
            pl.BlockSpec((None, kp, PROJ_BN), lambda i, j: (layer, 1, j)),
            pl.BlockSpec((PROJ_BM, PROJ_BN), lambda i, j: (i, j)),
        ],
        out_specs=pl.BlockSpec((PROJ_BM, PROJ_BN), lambda i, j: (i, j)),
        out_shape=jax.ShapeDtypeStruct((m, d), F32),
        compiler_params=_cparams(("arbitrary", "arbitrary"), VMEM_BUDGET),
        name="out_proj",
    )(a, p, w_all, w_all, x)


def _route_kernel(x_ref, g_ref, rw_ref, h_ref, idx_ref, gate_ref):
    x = x_ref[...]
    ms = jnp.mean(x * x, axis=-1, keepdims=True)
    h = x * lax.rsqrt(ms + NORM_EPS) * g_ref[...]
    h_hi, h_lo = _split_bf16(h)
    half = h.shape[1] // 2
    bits = lax.bitcast_convert_type(h_hi.astype(F32), jnp.uint32)
    h_ref[...] = (bits[:, half:] & jnp.uint32(0xFFFF0000)) | (bits[:, :half] >> 16)
    w_hi, w_lo = _split_bf16(rw_ref[...])
    logits = (jnp.dot(h_hi, w_hi, preferred_element_type=F32)
              + jnp.dot(h_hi, w_lo, preferred_element_type=F32)
              + jnp.dot(h_lo, w_hi, preferred_element_type=F32))
    lane = lax.broadcasted_iota(I32, logits.shape, 1)
    logits = jnp.where(lane < N_EXPERTS, logits, NEG_BIG)
    v1 = jnp.max(logits, axis=-1, keepdims=True)
    i1 = jnp.min(jnp.where(logits == v1, lane, LANES), axis=-1, keepdims=True)
    rest = jnp.where(lane == i1, NEG_BIG, logits)
    v2 = jnp.max(rest, axis=-1, keepdims=True)
    i2 = jnp.min(jnp.where(rest == v2, lane, LANES), axis=-1, keepdims=True)
    e = jnp.exp(v2 - v1)
    g1 = 1.0 / (1.0 + e)
    g2 = e / (1.0 + e)
    idx_ref[...] = jnp.where(lane == 0, i1, jnp.where(lane == 1, i2, 0))
    gate_ref[...] = jnp.where(lane == 0, g1, jnp.where(lane == 1, g2, 0.0))


def _route(x, g, rw_padded):
    m, d = x.shape
    bm = 256
    return pl.pallas_call(
        _route_kernel,
        grid=(m // bm,),
        in_specs=[
            pl.BlockSpec((bm, d), lambda i: (i, 0)),
            pl.BlockSpec((1, d), lambda i: (0, 0)),
            pl.BlockSpec((d, LANES), lambda i: (0, 0)),
        ],
        out_specs=[
            pl.BlockSpec((bm, d // 2), lambda i: (i, 0)),
            pl.BlockSpec((bm, LANES), lambda i: (i, 0)),
            pl.BlockSpec((bm, LANES), lambda i: (i, 0)),
        ],
        out_shape=[
            jax.ShapeDtypeStruct((m, d // 2), jnp.uint32),
            jax.ShapeDtypeStruct((m, LANES), I32),
            jax.ShapeDtypeStruct((m, LANES), F32),
        ],
        compiler_params=_cparams(("arbitrary",)),
        name="ffn_norm_route",
    )(x, g, rw_padded)


def _swiglu_kernel(eid_ref, nsub_ref, blk_ref, tot_ref, x_ref, w1_ref, w3_ref, w2_ref, *rest, has_res):
    if has_res:
        gain_ref, o_ref, w1s, w3s, w2s, h_ref, sem = rest
    else:
        o_ref, w1s, w3s, w2s = rest
        h_ref = x_ref
    t = pl.program_id(0)
    j = pl.program_id(1)
    nsub = nsub_ref[t]
    tile_rows = o_ref.shape[0]

    @pl.when(j == 0)
    def _():
        if has_res:
            seed = pltpu.make_async_copy(
                x_ref.at[pl.ds(pl.multiple_of(t * tile_rows, tile_rows), tile_rows)], o_ref, sem)
            seed.start()
            seed.wait()
            _rms_rows(o_ref, gain_ref, h_ref, tile_rows)
        else:
            def init(i, carry):
                rows = pl.ds(pl.multiple_of(i * FFN_SUB, FFN_SUB), FFN_SUB)
                o_ref[rows, :] = jnp.zeros((FFN_SUB, o_ref.shape[1]), F32)
                return carry
            lax.fori_loop(0, tile_rows // FFN_SUB, init, 0)

    def row_block(start, n_rows):
        rows = pl.ds(pl.multiple_of(start, FFN_SUB), n_rows)
        xs = h_ref[rows, :]
        a = jnp.dot(xs, w1s[...], preferred_element_type=F32)
        b = jnp.dot(xs, w3s[...], preferred_element_type=F32)
        hidden = (a * jax.nn.sigmoid(a) * b).astype(BF16)
        o_ref[rows, :] += jnp.dot(hidden, w2s[...], preferred_element_type=F32)

    def cast_weights():
        w1s[...] = w1_ref[0].astype(BF16)
        w3s[...] = w3_ref[0].astype(BF16)
        w2s[...] = w2_ref[0].astype(BF16)

    if has_res:
        cast_weights()
        for i in range(o_ref.shape[0] // FFN_MM):
            row_block(i * FFN_MM, FFN_MM)
    else:
        @pl.when(nsub > 0)
        def _():
            cast_weights()
            n_mm = nsub // (FFN_MM // FFN_SUB)
            n_trips = n_mm // FFN_TRIP

            def trip(i, carry):
                for k in range(FFN_TRIP):
                    row_block((i * FFN_TRIP + k) * FFN_MM, FFN_MM)
                return carry
            lax.fori_loop(0, n_trips, trip, 0)

            def single(i, carry):
                row_block(i * FFN_MM, FFN_MM)
                return carry
            lax.fori_loop(n_trips * FFN_TRIP, n_mm, single, 0)

            @pl.when(nsub % (FFN_MM // FFN_SUB) == 1)
            def _():
                row_block((nsub - 1) * FFN_SUB, FFN_SUB)


def _swiglu(meta, x, w1, w3, w2, gain, tile_rows):
    eid, nsub, blk, tot = meta
    n_tiles = eid.shape[0]
    d = x.shape[1]
    f = w1.shape[2]
    n_j = f // FFN_BF
    has_res = gain is not None

    def row_map(t, j, eid, nsub, blk, tot):
        return (blk[t], 0)

    def out_map(t, j, eid, nsub, blk, tot):
        return (t, 0)

    def hidden_block(t, j, tot):
        return jnp.where(t < tot[0], j, n_j - 1)

    def w13_map(t, j, eid, nsub, blk, tot):
        return (eid[t], 0, hidden_block(t, j, tot))

    def w2_map(t, j, eid, nsub, blk, tot):
        return (eid[t], hidden_block(t, j, tot), 0)

    once = pl.Buffered(1)
    in_specs = [
        pl.BlockSpec(memory_space=pl.ANY) if has_res else pl.BlockSpec((tile_rows, d), row_map, pipeline_mode=once),
        pl.BlockSpec((1, d, FFN_BF), w13_map),
        pl.BlockSpec((1, d, FFN_BF), w13_map),
        pl.BlockSpec((1, FFN_BF, d), w2_map),
    ]
    args = [x, w1, w3, w2]
    scratch = [
        pltpu.VMEM((d, FFN_BF), BF16),
        pltpu.VMEM((d, FFN_BF), BF16),
        pltpu.VMEM((FFN_BF, d), BF16),
    ]
    if has_res:
        in_specs.append(pl.BlockSpec((1, d), lambda t, j, *_: (0, 0)))
        args.append(gain)
        scratch.append(pltpu.VMEM((tile_rows, d), BF16))
        scratch.append(pltpu.SemaphoreType.DMA(()))
    return pl.pallas_call(
        functools.partial(_swiglu_kernel, has_res=has_res),
        grid_spec=pltpu.PrefetchScalarGridSpec(
            num_scalar_prefetch=4,
            grid=(n_tiles, n_j),
            in_specs=in_specs,
            out_specs=pl.BlockSpec((tile_rows, d), out_map, pipeline_mode=once),
            scratch_shapes=scratch,
        ),
        out_shape=jax.ShapeDtypeStruct((n_tiles * tile_rows, d), F32),
        compiler_params=_cparams(("arbitrary", "arbitrary"), VMEM_BUDGET),
        name="swiglu_res" if has_res else "swiglu_moe",
    )(eid, nsub, blk, tot, *args)


def _gather_kernel(nvalid_ref, tok_ref, tok_next_ref, h_hbm, o_ref, buf, sem):
    i = pl.program_id(0)
    nb = pl.num_programs(0)
    n = o_ref.shape[0]
    slot = i % 2
    nxt = jnp.minimum(i + 1, nb - 1)

    def row_copy(idx_ref, r, s):
        return pltpu.make_async_copy(h_hbm.at[pl.ds(idx_ref[0, 0, r], 1)], buf.at[s, pl.ds(r, 1)], sem.at[s])

    def start_all(idx_ref, s):
        def start(g, c):
            for k in range(DMA_UNROLL):
                row_copy(idx_ref, g * DMA_UNROLL + k, s).start(priority=k % 2)
            return c
        lax.fori_loop(0, n // DMA_UNROLL, start, 0)

    @pl.when((i == 0) & (nvalid_ref[0] > 0))
    def _():
        start_all(tok_ref, 0)

    @pl.when((i + 1 < nb) & (nvalid_ref[nxt] > 0))
    def _():
        start_all(tok_next_ref, 1 - slot)

    valid = nvalid_ref[i] > 0

    @pl.when(valid)
    def _():
        one_row = pltpu.make_async_copy(h_hbm.at[pl.ds(0, 1)], buf.at[slot, pl.ds(0, 1)], sem.at[slot])

        def wait(r, c):
            one_row.wait()
            return c
        lax.fori_loop(0, n, wait, 0, unroll=DMA_UNROLL)
        words = buf[slot]
        half = words.shape[1]
        o_ref[:, :half] = lax.bitcast_convert_type(words << 16, F32).astype(BF16)
        o_ref[:, half:] = lax.bitcast_convert_type(words & jnp.uint32(0xFFFF0000), F32).astype(BF16)

    @pl.when(jnp.logical_not(valid))
    def _():
        o_ref[...] = jnp.zeros(o_ref.shape, BF16)


def _gather_rows(nvalid_blocks, tok_of_row, h):
    n_rows = tok_of_row.shape[0]
    d = h.shape[1]
    nb = n_rows // GATHER_BM
    tok3 = tok_of_row.reshape(nb, 1, GATHER_BM)
    idx_block = (1, 1, GATHER_BM)
    return pl.pallas_call(
        _gather_kernel,
        grid_spec=pltpu.PrefetchScalarGridSpec(
            num_scalar_prefetch=1,
            grid=(nb,),
            in_specs=[
                pl.BlockSpec(idx_block, lambda i, nv: (i, 0, 0), memory_space=pltpu.SMEM),
                pl.BlockSpec(idx_block, lambda i, nv: (jnp.minimum(i + 1, nb - 1), 0, 0), memory_space=pltpu.SMEM),
                pl.BlockSpec(memory_space=pl.ANY),
            ],
            out_specs=pl.BlockSpec((GATHER_BM, 2 * d), lambda i, nv: (i, 0)),
            scratch_shapes=[pltpu.VMEM((2, GATHER_BM, d), jnp.uint32), pltpu.SemaphoreType.DMA((2,))],
        ),
        out_shape=jax.ShapeDtypeStruct((n_rows, 2 * d), BF16),
        compiler_params=_cparams(("arbitrary",)),
        name="moe_gather",
    )(nvalid_blocks, tok3, tok3, h)


def _combine_kernel(pos_ref, pos_next_ref, x_ref, gate_ref, y_hbm, o_ref, buf, sem):
    i = pl.program_id(0)
    nb = pl.num_programs(0)
    n = o_ref.shape[0]
    slot = i % 2

    def row_copy(idx_ref, r, k, s):
        return pltpu.make_async_copy(y_hbm.at[pl.ds(idx_ref[0, k, r], 1)], buf.at[s, k, pl.ds(r, 1)], sem.at[s])

    def start_all(idx_ref, s):
        def start(g, c):
            for k in range(DMA_UNROLL):
                r = g * DMA_UNROLL + k
                row_copy(idx_ref, r, 0, s).start(priority=0)
                row_copy(idx_ref, r, 1, s).start(priority=1)
            return c
        lax.fori_loop(0, n // DMA_UNROLL, start, 0)

    @pl.when(i == 0)
    def _():
        start_all(pos_ref, 0)

    @pl.when(i + 1 < nb)
    def _():
        start_all(pos_next_ref, 1 - slot)

    one_row = pltpu.make_async_copy(y_hbm.at[pl.ds(0, 1)], buf.at[slot, 0, pl.ds(0, 1)], sem.at[slot])

    def wait(r, c):
        one_row.wait()
        one_row.wait()
        return c
    lax.fori_loop(0, n, wait, 0, unroll=DMA_UNROLL)
    g = gate_ref[...]
    o_ref[...] = x_ref[...] + g[:, 0:1] * buf[slot, 0] + g[:, 1:2] * buf[slot, 1]


def _combine(pos, x, gate, y):
    m, d = x.shape
    nb = m // GATHER_BM
    pos3 = pos.reshape(nb, GATHER_BM, TOP_K).transpose(0, 2, 1)
    idx_block = (1, TOP_K, GATHER_BM)
    return pl.pallas_call(
        _combine_kernel,
        grid=(nb,),
        in_specs=[
            pl.BlockSpec(idx_block, lambda i: (i, 0, 0), memory_space=pltpu.SMEM),
            pl.BlockSpec(idx_block, lambda i: (jnp.minimum(i + 1, nb - 1), 0, 0), memory_space=pltpu.SMEM),
            pl.BlockSpec((GATHER_BM, d), lambda i: (i, 0)),
            pl.BlockSpec((GATHER_BM, LANES), lambda i: (i, 0)),
            pl.BlockSpec(memory_space=pl.ANY),
        ],
        out_specs=pl.BlockSpec((GATHER_BM, d), lambda i: (i, 0)),
        out_shape=jax.ShapeDtypeStruct((m, d), F32),
        scratch_shapes=[pltpu.VMEM((2, TOP_K, GATHER_BM, d), F32), pltpu.SemaphoreType.DMA((2,))],
        compiler_params=_cparams(("arbitrary",)),
        name="moe_combine",
    )(pos3, pos3, x, gate, y)


def _dispatch_plan(idx, n_tokens):
    n_assign = n_tokens * TOP_K
    n_tiles = n_assign // MOE_TILE + N_EXPERTS
    e_flat = idx.reshape(n_assign)
    onehot = (e_flat[:, None] == jnp.arange(N_EXPERTS, dtype=I32)[None, :]).astype(I32)
    counts = jnp.sum(onehot, axis=0)
    rank = jnp.take_along_axis(jnp.cumsum(onehot, axis=0) - onehot, e_flat[:, None], axis=1)[:, 0]
    tiles_e = (counts + MOE_TILE - 1) // MOE_TILE
    tile_end = jnp.cumsum(tiles_e)
    tile_start = tile_end - tiles_e
    total = tile_end[-1]
    pos = tile_start[e_flat] * MOE_TILE + rank
    t = jnp.arange(n_tiles, dtype=I32)
    t_eff = jnp.minimum(t, total - 1)
    eid = jnp.minimum(jnp.sum((t_eff[:, None] >= tile_end[None, :]).astype(I32), axis=1), N_EXPERTS - 1)
    rows_valid = jnp.clip(counts[eid] - (t_eff - tile_start[eid]) * MOE_TILE, 0, MOE_TILE)
    rows_valid = jnp.where(t < total, rows_valid, 0)
    nsub = (rows_valid + FFN_SUB - 1) // FFN_SUB
    sub_per_tile = MOE_TILE // GATHER_BM
    sub_id = jnp.arange(n_tiles * sub_per_tile, dtype=I32)
    nvalid_blocks = ((sub_id % sub_per_tile) < nsub[sub_id // sub_per_tile]).astype(I32)
    tok_of_row = jnp.zeros((n_tiles * MOE_TILE,), I32).at[pos].set(jnp.arange(n_assign, dtype=I32) // TOP_K)
    meta = (eid, nsub.astype(I32), t_eff, jnp.reshape(total, (1,)).astype(I32))
    return meta, nvalid_blocks, tok_of_row, pos.reshape(n_tokens, TOP_K)


def _moe(x, g, rw, w1_all, w3_all, w2_all, layer_idx):
    m, d = x.shape
    w1 = w1_all.reshape((-1,) + w1_all.shape[2:])
    w3 = w3_all.reshape((-1,) + w3_all.shape[2:])
    w2 = w2_all.reshape((-1,) + w2_all.shape[2:])
    rw_padded = jnp.zeros((d, LANES), F32).at[:, :N_EXPERTS].set(rw)
    h, idx, gate = _route(x, g, rw_padded)
    meta, nvalid_blocks, tok_of_row, pos = _dispatch_plan(idx[:, :TOP_K], m)
    meta = (meta[0] + layer_idx * N_EXPERTS,) + meta[1:]
    xs = _gather_rows(nvalid_blocks, tok_of_row, h)
    y = _swiglu(meta, xs, w1, w3, w2, None, MOE_TILE)
    return _combine(pos, x, gate, y)


def _dense(x, g, w1_all, w3_all, w2_all, layer_idx):
    m, d = x.shape
    n_tiles = m // DENSE_TILE
    meta = (jnp.full((n_tiles,), layer_idx, I32), jnp.full((n_tiles,), DENSE_TILE // FFN_SUB, I32),
            jnp.arange(n_tiles, dtype=I32), jnp.full((1,), n_tiles, I32))
    return _swiglu(meta, x, w1_all, w3_all, w2_all, g, DENSE_TILE)


def kernel(x, positions, attn_norm, w_in, q_norm, k_norm, lambda_q1, lambda_k1, lambda_q2, lambda_k2, subln, pool_w, pool_b, pool_scale, w_out, ffn_norm, dense_w1, dense_w3, dense_w2, router_w, moe_w1, moe_w3, moe_w2):
    batch, seq, d = x.shape
    depth = w_in.shape[0]
    m = batch * seq
    xf = x.reshape(m, d)
    pos = positions.reshape(m, 1)
    inv_freq = 1.0 / (ROPE_THETA ** (jnp.arange(0, HEAD_DIM, 2, dtype=F32) / HEAD_DIM))
    invf = jnp.tile(inv_freq, LANES // (HEAD_DIM // 2))[None, :]
    cos, sin_signed = _rope_tables(pos, invf)
    for l in range(depth):
        gain = jnp.concatenate([jnp.tile(q_norm[l], ATTN_WIDTH // HEAD_DIM),
                                jnp.tile(k_norm[l], ATTN_WIDTH // HEAD_DIM)])[None, :]
        qk, v, u = _in_proj(xf, attn_norm[l][None, :], w_in, l, cos, sin_signed, gain)
        lambda_init = 0.8 - 0.6 * math.exp(-0.3 * l)
        a = _attention(jnp.full((1,), lambda_init, F32), qk, v, lambda_q1[l][None, :], lambda_k1[l][None, :],
                       lambda_q2[l][None, :], lambda_k2[l][None, :], subln[l][:, None], batch, seq)
        p = _pool(u, pool_w, l, pool_b[l].reshape(1, POOL_WIDTH), pool_scale[l][None, :], seq)
        xf = _out_proj(a, p, w_out, l, xf)
        i = l // 2
        if l % 2 == 0:
            xf = _dense(xf, ffn_norm[l][None, :], dense_w1, dense_w3, dense_w2, i)
        else:
            xf = _moe(xf, ffn_norm[l][None, :], router_w[i], moe_w1, moe_w3, moe_w2, i)
    return xf.reshape(batch, seq, d)
```

```python
import functools
import math

import jax
import jax.numpy as jnp
from jax import lax
from jax.experimental import pallas as pl
from jax.experimental.pallas import tpu as pltpu

F32 = jnp.float32
BF16 = jnp.bfloat16
I32 = jnp.int32

D_MODEL = 2048
ATTN_WIDTH = 1024
POOL_WIDTH = 1024
HEAD_DIM = 64
N_HEADS = 8
HEAD_WIDTH = 2 * HEAD_DIM
POOL_WINDOWS = (2, 4, 8, 16)
POOL_GROUP_DIM = 256
ROPE_THETA = 10000.0
NORM_EPS = 1e-6
N_EXPERTS = 8
TOP_K = 2
NEG_BIG = -1e30
LOG2_E = math.log2(math.e)

LANES = 128
VMEM_BUDGET = 56 * 1024 * 1024

PROJ_BM = 2048
PROJ_BN = 512
IN_PROJ_BM = 2048
IN_PROJ_BN = 512
QK_SLAB = 512
X_CHUNK = 256
NORM_CHUNK = 128
ROPE_BM = 256
ATTN_BQ = 512
ONES_ROWS = 16
ATTN_STREAMS = 2
POOL_BM = 256
FFN_SUB = 256
FFN_MM = 512
FFN_TRIP = 4
FFN_BF = 256
DENSE_TILE = 2048
MOE_TILE = 2560
GATHER_BM = 256
DMA_UNROLL = 8


def _cparams(sem, vmem=None):
    return pltpu.CompilerParams(dimension_semantics=sem, vmem_limit_bytes=vmem)


def _split_bf16(x):
    hi = x.astype(BF16)
    lo = (x - hi.astype(F32)).astype(BF16)
    return hi, lo


def _rms_rows(x_ref, g_ref, h_ref, rows):
    def body(c, carry):
        r = pl.multiple_of(c * NORM_CHUNK, NORM_CHUNK)
        x = x_ref[pl.ds(r, NORM_CHUNK), :]
        ms = jnp.mean(x * x, axis=-1, keepdims=True)
        h_ref[pl.ds(r, NORM_CHUNK), :] = (x * lax.rsqrt(ms + NORM_EPS) * g_ref[...]).astype(h_ref.dtype)
        return carry
    lax.fori_loop(0, rows // NORM_CHUNK, body, 0)


def _in_proj_kernel(x_hbm, g_ref, w_ref, cos_ref, sin_ref, gain_ref, qk_ref, v_ref, u_ref, h_ref, xbuf, sem,
                    *, n_qk, n_v):
    i = pl.program_id(0)
    j = pl.program_id(1)
    bm = h_ref.shape[0]
    chunk = xbuf.shape[1]
    n_chunks = bm // chunk

    @pl.when(j == 0)
    def _():
        def chunk_copy(c, slot):
            rows = pl.ds(pl.multiple_of(i * bm + c * chunk, chunk), chunk)
            return pltpu.make_async_copy(x_hbm.at[rows], xbuf.at[slot], sem.at[slot])

        chunk_copy(0, 0).start()
        for c in range(n_chunks):
            slot = c % 2
            if c + 1 < n_chunks:
                chunk_copy(c + 1, 1 - slot).start()
            chunk_copy(c, slot).wait()
            _rms_rows(xbuf.at[slot], g_ref, h_ref.at[pl.ds(c * chunk, chunk)], chunk)

    def proj():
        return jnp.dot(h_ref[...], w_ref[...].astype(BF16), preferred_element_type=F32)

    def qk_epilogue(y, rows):
        lane = lax.broadcasted_iota(I32, (1, LANES), 1)
        first_half = (lane % HEAD_DIM) < (HEAD_DIM // 2)
        wide = 2 * LANES
        gr = lax.broadcasted_iota(I32, (wide, wide), 0) // HEAD_DIM
        gc = lax.broadcasted_iota(I32, (wide, wide), 1) // HEAD_DIM
        group_ones = jnp.where(gr == gc, 1.0, 0.0).astype(BF16)
        scale = jnp.where(j < n_qk // 2, HEAD_DIM ** -0.5 * LOG2_E, 1.0)
        cos = cos_ref[rows, :]
        sin_signed = sin_ref[rows, :]
        for wb in range(y.shape[1] // wide):
            x2 = y[:, wb * wide:(wb + 1) * wide]
            ssum2 = jnp.dot((x2 * x2).astype(BF16), group_ones, preferred_element_type=F32)
            for hb in range(2):
                cols = slice(wb * wide + hb * LANES, wb * wide + (hb + 1) * LANES)
                x = x2[:, hb * LANES:(hb + 1) * LANES]
                ssum = ssum2[:, hb * LANES:(hb + 1) * LANES]
                z = x * lax.rsqrt(ssum * (1.0 / HEAD_DIM) + NORM_EPS) * gain_ref[:, cols]
                swapped = jnp.where(first_half, pltpu.roll(z, LANES - HEAD_DIM // 2, 1),
                                    pltpu.roll(z, HEAD_DIM // 2, 1))
                qk_ref[rows, cols] = ((z * cos + swapped * sin_signed) * scale).astype(BF16)

    @pl.when(j < n_qk)
    def _():
        w = w_ref[...].astype(BF16)
        for r0 in range(0, bm, QK_SLAB):
            rows = slice(r0, r0 + QK_SLAB)
            qk_epilogue(jnp.dot(h_ref[rows, :], w, preferred_element_type=F32), rows)

    @pl.when((j >= n_qk) & (j < n_qk + n_v))
    def _():
        v_ref[...] = proj().astype(BF16)

    @pl.when(j >= n_qk + n_v)
    def _():
        u_ref[...] = proj()


def _in_proj(x, g, w_all, layer, cos, sin_signed, qk_gain):
    m, d = x.shape
    bn = IN_PROJ_BN
    bm = IN_PROJ_BM
    n_qk = 2 * ATTN_WIDTH // bn
    n_v = ATTN_WIDTH // bn
    n_u = POOL_WIDTH // bn
    assert w_all.shape[2] == (n_qk + n_v + n_u) * bn and m % bm == 0
    return pl.pallas_call(
        functools.partial(_in_proj_kernel, n_qk=n_qk, n_v=n_v),
        grid=(m // bm, n_qk + n_v + n_u),
        in_specs=[
            pl.BlockSpec(memory_space=pl.ANY),
            pl.BlockSpec((1, d), lambda i, j: (0, 0)),
            pl.BlockSpec((None, d, bn), lambda i, j: (layer, 0, j)),
            pl.BlockSpec((bm, LANES), lambda i, j: (i, 0)),
            pl.BlockSpec((bm, LANES), lambda i, j: (i, 0)),
            pl.BlockSpec((1, bn), lambda i, j: (0, jnp.minimum(j, n_qk - 1))),
        ],
        out_specs=[
            pl.BlockSpec((bm, bn), lambda i, j: (i, jnp.minimum(j, n_qk - 1))),
            pl.BlockSpec((bm, bn), lambda i, j: (i, jnp.clip(j - n_qk, 0, n_v - 1))),
            pl.BlockSpec((bm, bn), lambda i, j: (i, jnp.clip(j - n_qk - n_v, 0, n_u - 1))),
        ],
        out_shape=[
            jax.ShapeDtypeStruct((m, 2 * ATTN_WIDTH), BF16),
            jax.ShapeDtypeStruct((m, ATTN_WIDTH), BF16),
            jax.ShapeDtypeStruct((m, POOL_WIDTH), F32),
        ],
        scratch_shapes=[
            pltpu.VMEM((bm, d), BF16),
            pltpu.VMEM((2, X_CHUNK, d), F32),
            pltpu.SemaphoreType.DMA((2,)),
        ],
        compiler_params=_cparams(("arbitrary", "arbitrary"), VMEM_BUDGET),
        name="in_proj",
    )(x, g, w_all, cos, sin_signed, qk_gain)


def _rope_tables_kernel(pos_ref, invf_ref, cos_ref, sin_ref):
    ang = pos_ref[...].astype(F32) * invf_ref[...]
    lane = lax.broadcasted_iota(I32, ang.shape, 1)
    first_half = (lane % HEAD_DIM) < (HEAD_DIM // 2)
    sin = jnp.sin(ang)
    cos_ref[...] = jnp.cos(ang)
    sin_ref[...] = jnp.where(first_half, -sin, sin)


def _rope_tables(pos, invf):
    m = pos.shape[0]
    table = jax.ShapeDtypeStruct((m, LANES), F32)
    return pl.pallas_call(
        _rope_tables_kernel,
        grid=(m // ROPE_BM,),
        in_specs=[
            pl.BlockSpec((ROPE_BM, 1), lambda i: (i, 0)),
            pl.BlockSpec((1, LANES), lambda i: (0, 0)),
        ],
        out_specs=[pl.BlockSpec((ROPE_BM, LANES), lambda i: (i, 0))] * 2,
        out_shape=[table, table],
        compiler_params=_cparams(("arbitrary",)),
        name="rope_tables",
    )(pos, invf)


def _block_streams(nq):
    chains = [[(qi, kb) for kb in range(qi + 1)] for qi in range(nq - 1, -1, -1)]
    streams = tuple([] for _ in range(ATTN_STREAMS))
    for chain in chains:
        min(streams, key=len).extend(chain)
    order = []
    for i in range(max(len(s) for s in streams)):
        order.extend(s[i] for s in streams if i < len(s))
    return order


def _attn_kernel(linit_ref, q_ref, k_ref, v_ref, lq1_ref, lk1_ref, lq2_ref, lk2_ref, sg_ref, o_ref,
                 q2_ref, vt_ref, m_ref, acc_ref):
    nq, _, bq = vt_ref.shape
    lane = lax.broadcasted_iota(I32, (bq, HEAD_WIDTH), 1)
    for c in range(nq):
        rows = slice(c * bq, (c + 1) * bq)
        vt_ref[c, 0:HEAD_WIDTH, :] = v_ref[rows, :].astype(F32).T.astype(BF16)
        vt_ref[c, HEAD_WIDTH:, :] = jnp.ones((ONES_ROWS, bq), BF16)
        q = q_ref[rows, :]
        zero = jnp.zeros_like(q)
        q2_ref[c, 0:bq, :] = jnp.where(lane < HEAD_DIM, q, zero)
        q2_ref[c, bq:2 * bq, :] = jnp.where(lane >= HEAD_DIM, q, zero)
    m_ref[...] = jnp.full(m_ref.shape, NEG_BIG, F32)
    acc_ref[...] = jnp.zeros(acc_ref.shape, F32)

    for qi, kb in _block_streams(nq):
        k = k_ref[kb * bq:(kb + 1) * bq, :]
        s = lax.dot_general(k, q2_ref[qi], (((1,), (1,)), ((), ())), preferred_element_type=F32)
        if kb == qi:
            key = lax.broadcasted_iota(I32, s.shape, 0)
            qry = lax.broadcasted_iota(I32, s.shape, 1) & (bq - 1)
            s = jnp.where(key <= qry, s, NEG_BIG)
        m_old = m_ref[qi]
        m_new = jnp.maximum(m_old, jnp.max(s, axis=0, keepdims=True))
        alpha = jnp.exp2(m_old - m_new)
        p = jnp.exp2(s - m_new)
        acc_ref[qi] = alpha * acc_ref[qi] + jnp.dot(vt_ref[kb], p.astype(BF16), preferred_element_type=F32)
        m_ref[qi] = m_new

    lambda_init = linit_ref[0]
    lam = (jnp.exp(jnp.sum(lq1_ref[...] * lk1_ref[...], axis=-1, keepdims=True))
           - jnp.exp(jnp.sum(lq2_ref[...] * lk2_ref[...], axis=-1, keepdims=True))
           + lambda_init)
    for qi in range(nq):
        acc = acc_ref[qi, 0:HEAD_WIDTH, :]
        l = acc_ref[qi, HEAD_WIDTH:HEAD_WIDTH + 1, :]
        o = acc[:, :bq] / l[:, :bq] - lam * (acc[:, bq:] / l[:, bq:])
        ms = jnp.mean(o * o, axis=0, keepdims=True)
        o = o * lax.rsqrt(ms + NORM_EPS) * sg_ref[...] * (1.0 - lambda_init)
        o_ref[qi * bq:(qi + 1) * bq, :] = o.T.astype(BF16)


def _attention(linit, qk, v, lq1, lk1, lq2, lk2, sg_col, batch, seq):
    m = qk.shape[0]
    nq = seq // ATTN_BQ
    vec = pl.BlockSpec((1, HEAD_DIM), lambda b, h: (0, 0))
    return pl.pallas_call(
        _attn_kernel,
        grid=(batch, N_HEADS),
        in_specs=[
            pl.BlockSpec(memory_space=pltpu.SMEM),
            pl.BlockSpec((seq, HEAD_WIDTH), lambda b, h: (b, h)),
            pl.BlockSpec((seq, HEAD_WIDTH), lambda b, h: (b, N_HEADS + h)),
            pl.BlockSpec((seq, HEAD_WIDTH), lambda b, h: (b, h)),
            vec, vec, vec, vec,
            pl.BlockSpec((HEAD_WIDTH, 1), lambda b, h: (0, 0)),
        ],
        out_specs=pl.BlockSpec((seq, HEAD_WIDTH), lambda b, h: (b, h)),
        out_shape=jax.ShapeDtypeStruct((m, ATTN_WIDTH), BF16),
        scratch_shapes=[
            pltpu.VMEM((nq, 2 * ATTN_BQ, HEAD_WIDTH), BF16),
            pltpu.VMEM((nq, HEAD_WIDTH + ONES_ROWS, ATTN_BQ), BF16),
            pltpu.VMEM((nq, 1, 2 * ATTN_BQ), F32),
            pltpu.VMEM((nq, HEAD_WIDTH + ONES_ROWS, 2 * ATTN_BQ), F32),
        ],
        compiler_params=_cparams(("arbitrary", "arbitrary")),
        name="diff_attn",
    )(linit, qk, qk, v, lq1, lk1, lq2, lk2, sg_col)


def _pool_kernel(uc_ref, up_ref, w_ref, b_ref, sc_ref, o_ref, *, chunks_per_seq):
    c = pl.program_id(0) % chunks_per_seq
    bm = uc_ref.shape[0]
    row = lax.broadcasted_iota(I32, (bm, bm), 0)
    col = lax.broadcasted_iota(I32, (bm, bm), 1)
    t = c * bm + lax.broadcasted_iota(I32, (bm, 1), 0)
    has_prev = c > 0
    halo = max(POOL_WINDOWS)
    hrow = lax.broadcasted_iota(I32, (halo, halo), 0)
    hcol = lax.broadcasted_iota(I32, (halo, halo), 1)
    for g, win in enumerate(POOL_WINDOWS):
        cols = slice(g * POOL_GROUP_DIM, (g + 1) * POOL_GROUP_DIM)
        cur = jnp.where((row >= col) & (row - col < win), 1.0, 0.0).astype(BF16)
        prv = jnp.where(hcol - hrow > halo - win, 1.0, 0.0).astype(BF16)
        u = uc_ref[:, cols]
        u_hi, u_lo = _split_bf16(u)
        p_hi, p_lo = _split_bf16(up_ref[bm - halo:, cols])
        wsum = (jnp.dot(cur, u_hi, preferred_element_type=F32)
                + jnp.dot(cur, u_lo, preferred_element_type=F32))
        wprev = (jnp.dot(prv, p_hi, preferred_element_type=F32)
                 + jnp.dot(prv, p_lo, preferred_element_type=F32))
        wsum = jnp.concatenate([wsum[:halo] + jnp.where(has_prev, wprev, 0.0), wsum[halo:]], axis=0)
        cnt = jnp.minimum(t + 1, win).astype(F32)
        d = wsum / cnt - u
        y = jnp.dot(d.astype(BF16), w_ref[g].astype(BF16), preferred_element_type=F32) + b_ref[:, cols]
        o_ref[:, cols] = (y * sc_ref[:, cols]).astype(BF16)


def _pool(u, w_all, layer, b, sc, seq):
    m, width = u.shape
    cps = seq // POOL_BM
    return pl.pallas_call(
        functools.partial(_pool_kernel, chunks_per_seq=cps),
        grid=(m // POOL_BM,),
        in_specs=[
            pl.BlockSpec((POOL_BM, width), lambda i: (i, 0)),
            pl.BlockSpec((POOL_BM, width), lambda i: (jnp.maximum(i - 1, 0), 0)),
            pl.BlockSpec((None,) + w_all.shape[1:], lambda i: (layer, 0, 0, 0)),
            pl.BlockSpec((1, width), lambda i: (0, 0)),
            pl.BlockSpec((1, width), lambda i: (0, 0)),
        ],
        out_specs=pl.BlockSpec((POOL_BM, width), lambda i: (i, 0)),
        out_shape=jax.ShapeDtypeStruct((m, width), BF16),
        compiler_params=_cparams(("arbitrary",)),
        name="pool_mixer",
    )(u, u, w_all, b, sc)


def _out_proj_kernel(a_ref, p_ref, wa_ref, wp_ref, x_ref, o_ref):
    y = jnp.dot(a_ref[...], wa_ref[...].astype(BF16), preferred_element_type=F32)
    y = y + jnp.dot(p_ref[...], wp_ref[...].astype(BF16), preferred_element_type=F32)
    o_ref[...] = x_ref[...] + y


def _out_proj(a, p, w_all, layer, x):
    m, d = x.shape
    ka = a.shape[1]
    kp = p.shape[1]
    assert ka == kp
    return pl.pallas_call(
        _out_proj_kernel,
        grid=(m // PROJ_BM, d // PROJ_BN),
        in_specs=[
            pl.BlockSpec((PROJ_BM, ka), lambda i, j: (i, 0)),
            pl.BlockSpec((PROJ_BM, kp), lambda i, j: (i, 0)),
            pl.BlockSpec((None, ka, PROJ_BN), lambda i, j: (layer, 0, j)),
            pl.BlockSpec((None, kp, PROJ_BN), lambda i, j: (layer, 1, j)),
            pl.BlockSpec((PROJ_BM, PROJ_BN), lambda i, j: (i, j)),
        ],
        out_specs=pl.BlockSpec((PROJ_BM, PROJ_BN), lambda i, j: (i, j)),
        out_shape=jax.ShapeDtypeStruct((m, d), F32),
        compiler_params=_cparams(("arbitrary", "arbitrary"), VMEM_BUDGET),
        name="out_proj",
    )(a, p, w_all, w_all, x)


def _route_kernel(x_ref, g_ref, rw_ref, h_ref, idx_ref, gate_ref):
    x = x_ref[...]
    ms = jnp.mean(x * x, axis=-1, keepdims=True)
    h = x * lax.rsqrt(ms + NORM_EPS) * g_ref[...]
    h_hi, h_lo = _split_bf16(h)
    half = h.shape[1] // 2
    bits = lax.bitcast_convert_type(h_hi.astype(F32), jnp.uint32)
    h_ref[...] = (bits[:, half:] & jnp.uint32(0xFFFF0000)) | (bits[:, :half] >> 16)
    w_hi, w_lo = _split_bf16(rw_ref[...])
    logits = (jnp.dot(h_hi, w_hi, preferred_element_type=F32)
              + jnp.dot(h_hi, w_lo, preferred_element_type=F32)
              + jnp.dot(h_lo, w_hi, preferred_element_type=F32))
    lane = lax.broadcasted_iota(I32, logits.shape, 1)
    logits = jnp.where(lane < N_EXPERTS, logits, NEG_BIG)
    v1 = jnp.max(logits, axis=-1, keepdims=True)
    i1 = jnp.min(jnp.where(logits == v1, lane, LANES), axis=-1, keepdims=True)
    rest = jnp.where(lane == i1, NEG_BIG, logits)
    v2 = jnp.max(rest, axis=-1, keepdims=True)
    i2 = jnp.min(jnp.where(rest == v2, lane, LANES), axis=-1, keepdims=True)
    e = jnp.exp(v2 - v1)
    g1 = 1.0 / (1.0 + e)
    g2 = e / (1.0 + e)
    idx_ref[...] = jnp.where(lane == 0, i1, jnp.where(lane == 1, i2, 0))
    gate_ref[...] = jnp.where(lane == 0, g1, jnp.where(lane == 1, g2, 0.0))


def _route(x, g, rw_padded):
    m, d = x.shape
    bm = 256
    return pl.pallas_call(
        _route_kernel,
        grid=(m // bm,),
        in_specs=[
            pl.BlockSpec((bm, d), lambda i: (i, 0)),
            pl.BlockSpec((1, d), lambda i: (0, 0)),
            pl.BlockSpec((d, LANES), lambda i: (0, 0)),
        ],
        out_specs=[
            pl.BlockSpec((bm, d // 2), lambda i: (i, 0)),
            pl.BlockSpec((bm, LANES), lambda i: (i, 0)),
            pl.BlockSpec((bm, LANES), lambda i: (i, 0)),
        ],
        out_shape=[
            jax.ShapeDtypeStruct((m, d // 2), jnp.uint32),
            jax.ShapeDtypeStruct((m, LANES), I32),
            jax.ShapeDtypeStruct((m, LANES), F32),
        ],
        compiler_params=_cparams(("arbitrary",)),
        name="ffn_norm_route",
    )(x, g, rw_padded)


def _swiglu_kernel(eid_ref, nsub_ref, blk_ref, tot_ref, x_ref, w1_ref, w3_ref, w2_ref, *rest, has_res):
    if has_res:
        gain_ref, o_ref, w1s, w3s, w2s, h_ref, sem = rest
    else:
        o_ref, w1s, w3s, w2s = rest
        h_ref = x_ref
    t = pl.program_id(0)
    j = pl.program_id(1)
    nsub = nsub_ref[t]
    tile_rows = o_ref.shape[0]

    @pl.when(j == 0)
    def _():
        if has_res:
            seed = pltpu.make_async_copy(
                x_ref.at[pl.ds(pl.multiple_of(t * tile_rows, tile_rows), tile_rows)], o_ref, sem)
            seed.start()
            seed.wait()
            _rms_rows(o_ref, gain_ref, h_ref, tile_rows)
        else:
            def init(i, carry):
                rows = pl.ds(pl.multiple_of(i * FFN_SUB, FFN_SUB), FFN_SUB)
                o_ref[rows, :] = jnp.zeros((FFN_SUB, o_ref.shape[1]), F32)
                return carry
            lax.fori_loop(0, tile_rows // FFN_SUB, init, 0)

    def row_block(start, n_rows):
        rows = pl.ds(pl.multiple_of(start, FFN_SUB), n_rows)
        xs = h_ref[rows, :]
        a = jnp.dot(xs, w1s[...], preferred_element_type=F32)
        b = jnp.dot(xs, w3s[...], preferred_element_type=F32)
        hidden = (a * jax.nn.sigmoid(a) * b).astype(BF16)
        o_ref[rows, :] += jnp.dot(hidden, w2s[...], preferred_element_type=F32)

    def cast_weights():
        w1s[...] = w1_ref[0].astype(BF16)
        w3s[...] = w3_ref[0].astype(BF16)
        w2s[...] = w2_ref[0].astype(BF16)

    if has_res:
        cast_weights()
        for i in range(o_ref.shape[0] // FFN_MM):
            row_block(i * FFN_MM, FFN_MM)
    else:
        @pl.when(nsub > 0)
        def _():
            cast_weights()
            n_mm = nsub // (FFN_MM // FFN_SUB)
            n_trips = n_mm // FFN_TRIP

            def trip(i, carry):
                for k in range(FFN_TRIP):
                    row_block((i * FFN_TRIP + k) * FFN_MM, FFN_MM)
                return carry
            lax.fori_loop(0, n_trips, trip, 0)

            def single(i, carry):
                row_block(i * FFN_MM, FFN_MM)
                return carry
            lax.fori_loop(n_trips * FFN_TRIP, n_mm, single, 0)

            @pl.when(nsub % (FFN_MM // FFN_SUB) == 1)
            def _():
                row_block((nsub - 1) * FFN_SUB, FFN_SUB)


def _swiglu(meta, x, w1, w3, w2, gain, tile_rows):
    eid, nsub, blk, tot = meta
    n_tiles = eid.shape[0]
    d = x.shape[1]
    f = w1.shape[2]
    n_j = f // FFN_BF
    has_res = gain is not None

    def row_map(t, j, eid, nsub, blk, tot):
        return (blk[t], 0)

    def out_map(t, j, eid, nsub, blk, tot):
        return (t, 0)

    def hidden_block(t, j, tot):
        return jnp.where(t < tot[0], j, n_j - 1)

    def w13_map(t, j, eid, nsub, blk, tot):
        return (eid[t], 0, hidden_block(t, j, tot))

    def w2_map(t, j, eid, nsub, blk, tot):
        return (eid[t], hidden_block(t, j, tot), 0)

    once = pl.Buffered(1)
    in_specs = [
        pl.BlockSpec(memory_space=pl.ANY) if has_res else pl.BlockSpec((tile_rows, d), row_map, pipeline_mode=once),
        pl.BlockSpec((1, d, FFN_BF), w13_map),
        pl.BlockSpec((1, d, FFN_BF), w13_map),
        pl.BlockSpec((1, FFN_BF, d), w2_map),
    ]
    args = [x, w1, w3, w2]
    scratch = [
        pltpu.VMEM((d, FFN_BF), BF16),
        pltpu.VMEM((d, FFN_BF), BF16),
        pltpu.VMEM((FFN_BF, d), BF16),
    ]
    if has_res:
        in_specs.append(pl.BlockSpec((1, d), lambda t, j, *_: (0, 0)))
        args.append(gain)
        scratch.append(pltpu.VMEM((tile_rows, d), BF16))
        scratch.append(pltpu.SemaphoreType.DMA(()))
    return pl.pallas_call(
        functools.partial(_swiglu_kernel, has_res=has_res),
        grid_spec=pltpu.PrefetchScalarGridSpec(
            num_scalar_prefetch=4,
            grid=(n_tiles, n_j),
            in_specs=in_specs,
            out_specs=pl.BlockSpec((tile_rows, d), out_map, pipeline_mode=once),
            scratch_shapes=scratch,
        ),
        out_shape=jax.ShapeDtypeStruct((n_tiles * tile_rows, d), F32),
        compiler_params=_cparams(("arbitrary", "arbitrary"), VMEM_BUDGET),
        name="swiglu_res" if has_res else "swiglu_moe",
    )(eid, nsub, blk, tot, *args)


def _gather_kernel(nvalid_ref, tok_ref, h_ref, o_ref, buf):
    i = pl.program_id(0)
    n = o_ref.shape[0]
    valid = nvalid_ref[i] > 0

    @pl.when(valid)
    def _():
        def pick(g, c):
            for k in range(DMA_UNROLL):
                r = g * DMA_UNROLL + k
                buf[pl.ds(r, 1), :] = h_ref[pl.ds(tok_ref[0, 0, r], 1), :]
            return c
        lax.fori_loop(0, n // DMA_UNROLL, pick, 0)
        words = buf[...]
        half = words.shape[1]
        o_ref[:, :half] = lax.bitcast_convert_type(words << 16, F32).astype(BF16)
        o_ref[:, half:] = lax.bitcast_convert_type(words & jnp.uint32(0xFFFF0000), F32).astype(BF16)

    @pl.when(jnp.logical_not(valid))
    def _():
        o_ref[...] = jnp.zeros(o_ref.shape, BF16)


def _gather_rows(nvalid_blocks, tok_of_row, h):
    n_rows = tok_of_row.shape[0]
    m, d = h.shape
    nb = n_rows // GATHER_BM
    tok3 = tok_of_row.reshape(nb, 1, GATHER_BM)
    return pl.pallas_call(
        _gather_kernel,
        grid_spec=pltpu.PrefetchScalarGridSpec(
            num_scalar_prefetch=1,
            grid=(nb,),
            in_specs=[
                pl.BlockSpec((1, 1, GATHER_BM), lambda i, nv: (i, 0, 0), memory_space=pltpu.SMEM),
                pl.BlockSpec((m, d), lambda i, nv: (0, 0), pipeline_mode=pl.Buffered(1)),
            ],
            out_specs=pl.BlockSpec((GATHER_BM, 2 * d), lambda i, nv: (i, 0)),
            scratch_shapes=[pltpu.VMEM((GATHER_BM, d), jnp.uint32)],
        ),
        out_shape=jax.ShapeDtypeStruct((n_rows, 2 * d), BF16),
        compiler_params=_cparams(("arbitrary",), VMEM_BUDGET),
        name="moe_gather",
    )(nvalid_blocks, tok3, h)


def _combine_kernel(pos_ref, pos_next_ref, x_ref, gate_ref, y_hbm, o_ref, buf, sem):
    i = pl.program_id(0)
    nb = pl.num_programs(0)
    n = o_ref.shape[0]
    slot = i % 2

    def row_copy(idx_ref, r, k, s):
        return pltpu.make_async_copy(y_hbm.at[pl.ds(idx_ref[0, k, r], 1)], buf.at[s, k, pl.ds(r, 1)], sem.at[s])

    def start_all(idx_ref, s):
        def start(g, c):
            for k in range(DMA_UNROLL):
                r = g * DMA_UNROLL + k
                row_copy(idx_ref, r, 0, s).start(priority=0)
                row_copy(idx_ref, r, 1, s).start(priority=1)
            return c
        lax.fori_loop(0, n // DMA_UNROLL, start, 0)

    @pl.when(i == 0)
    def _():
        start_all(pos_ref, 0)

    @pl.when(i + 1 < nb)
    def _():
        start_all(pos_next_ref, 1 - slot)

    one_row = pltpu.make_async_copy(y_hbm.at[pl.ds(0, 1)], buf.at[slot, 0, pl.ds(0, 1)], sem.at[slot])

    def wait(r, c):
        one_row.wait()
        one_row.wait()
        return c
    lax.fori_loop(0, n, wait, 0, unroll=DMA_UNROLL)
    g = gate_ref[...]
    o_ref[...] = x_ref[...] + g[:, 0:1] * buf[slot, 0] + g[:, 1:2] * buf[slot, 1]


def _combine(pos, x, gate, y):
    m, d = x.shape
    nb = m // GATHER_BM
    pos3 = pos.reshape(nb, GATHER_BM, TOP_K).transpose(0, 2, 1)
    idx_block = (1, TOP_K, GATHER_BM)
    return pl.pallas_call(
        _combine_kernel,
        grid=(nb,),
        in_specs=[
            pl.BlockSpec(idx_block, lambda i: (i, 0, 0), memory_space=pltpu.SMEM),
            pl.BlockSpec(idx_block, lambda i: (jnp.minimum(i + 1, nb - 1), 0, 0), memory_space=pltpu.SMEM),
            pl.BlockSpec((GATHER_BM, d), lambda i: (i, 0)),
            pl.BlockSpec((GATHER_BM, LANES), lambda i: (i, 0)),
            pl.BlockSpec(memory_space=pl.ANY),
        ],
        out_specs=pl.BlockSpec((GATHER_BM, d), lambda i: (i, 0)),
        out_shape=jax.ShapeDtypeStruct((m, d), F32),
        scratch_shapes=[pltpu.VMEM((2, TOP_K, GATHER_BM, d), F32), pltpu.SemaphoreType.DMA((2,))],
        compiler_params=_cparams(("arbitrary",)),
        name="moe_combine",
    )(pos3, pos3, x, gate, y)


def _dispatch_plan(idx, n_tokens):
    n_assign = n_tokens * TOP_K
    n_tiles = n_assign // MOE_TILE + N_EXPERTS
    e_flat = idx.reshape(n_assign)
    onehot = (e_flat[:, None] == jnp.arange(N_EXPERTS, dtype=I32)[None, :]).astype(I32)
    counts = jnp.sum(onehot, axis=0)
    rank = jnp.take_along_axis(jnp.cumsum(onehot, axis=0) - onehot, e_flat[:, None], axis=1)[:, 0]
    tiles_e = (counts + MOE_TILE - 1) // MOE_TILE
    tile_end = jnp.cumsum(tiles_e)
    tile_start = tile_end - tiles_e
    total = tile_end[-1]
    pos = tile_start[e_flat] * MOE_TILE + rank
    t = jnp.arange(n_tiles, dtype=I32)
    t_eff = jnp.minimum(t, total - 1)
    eid = jnp.minimum(jnp.sum((t_eff[:, None] >= tile_end[None, :]).astype(I32), axis=1), N_EXPERTS - 1)
    rows_valid = jnp.clip(counts[eid] - (t_eff - tile_start[eid]) * MOE_TILE, 0, MOE_TILE)
    rows_valid = jnp.where(t < total, rows_valid, 0)
    nsub = (rows_valid + FFN_SUB - 1) // FFN_SUB
    sub_per_tile = MOE_TILE // GATHER_BM
    sub_id = jnp.arange(n_tiles * sub_per_tile, dtype=I32)
    nvalid_blocks = ((sub_id % sub_per_tile) < nsub[sub_id // sub_per_tile]).astype(I32)
    tok_of_row = jnp.zeros((n_tiles * MOE_TILE,), I32).at[pos].set(jnp.arange(n_assign, dtype=I32) // TOP_K)
    meta = (eid, nsub.astype(I32), t_eff, jnp.reshape(total, (1,)).astype(I32))
    return meta, nvalid_blocks, tok_of_row, pos.reshape(n_tokens, TOP_K)


def _moe(x, g, rw, w1_all, w3_all, w2_all, layer_idx):
    m, d = x.shape
    w1 = w1_all.reshape((-1,) + w1_all.shape[2:])
    w3 = w3_all.reshape((-1,) + w3_all.shape[2:])
    w2 = w2_all.reshape((-1,) + w2_all.shape[2:])
    rw_padded = jnp.zeros((d, LANES), F32).at[:, :N_EXPERTS].set(rw)
    h, idx, gate = _route(x, g, rw_padded)
    meta, nvalid_blocks, tok_of_row, pos = _dispatch_plan(idx[:, :TOP_K], m)
    meta = (meta[0] + layer_idx * N_EXPERTS,) + meta[1:]
    xs = _gather_rows(nvalid_blocks, tok_of_row, h)
    y = _swiglu(meta, xs, w1, w3, w2, None, MOE_TILE)
    return _combine(pos, x, gate, y)


def _dense(x, g, w1_all, w3_all, w2_all, layer_idx):
    m, d = x.shape
    n_tiles = m // DENSE_TILE
    meta = (jnp.full((n_tiles,), layer_idx, I32), jnp.full((n_tiles,), DENSE_TILE // FFN_SUB, I32),
            jnp.arange(n_tiles, dtype=I32), jnp.full((1,), n_tiles, I32))
    return _swiglu(meta, x, w1_all, w3_all, w2_all, g, DENSE_TILE)


def kernel(x, positions, attn_norm, w_in, q_norm, k_norm, lambda_q1, lambda_k1, lambda_q2, lambda_k2, subln, pool_w, pool_b, pool_scale, w_out, ffn_norm, dense_w1, dense_w3, dense_w2, router_w, moe_w1, moe_w3, moe_w2):
    batch, seq, d = x.shape
    depth = w_in.shape[0]
    m = batch * seq
    xf = x.reshape(m, d)
    pos = positions.reshape(m, 1)
    inv_freq = 1.0 / (ROPE_THETA ** (jnp.arange(0, HEAD_DIM, 2, dtype=F32) / HEAD_DIM))
    invf = jnp.tile(inv_freq, LANES // (HEAD_DIM // 2))[None, :]
    cos, sin_signed = _rope_tables(pos, invf)
    for l in range(depth):
        gain = jnp.concatenate([jnp.tile(q_norm[l], ATTN_WIDTH // HEAD_DIM),
                                jnp.tile(k_norm[l], ATTN_WIDTH // HEAD_DIM)])[None, :]
        qk, v, u = _in_proj(xf, attn_norm[l][None, :], w_in, l, cos, sin_signed, gain)
        lambda_init = 0.8 - 0.6 * math.exp(-0.3 * l)
        a = _attention(jnp.full((1,), lambda_init, F32), qk, v, lambda_q1[l][None, :], lambda_k1[l][None, :],
                       lambda_q2[l][None, :], lambda_k2[l][None, :], subln[l][:, None], batch, seq)
        p = _pool(u, pool_w, l, pool_b[l].reshape(1, POOL_WIDTH), pool_scale[l][None, :], seq)
        xf = _out_proj(a, p, w_out, l, xf)
        i = l // 2
        if l % 2 == 0:
            xf = _dense(xf, ffn_norm[l][None, :], dense_w1, dense_w3, dense_w2, i)
        else:
            xf = _moe(xf, ffn_norm[l][None, :], router_w[i], moe_w1, moe_w3, moe_w2, i)
    return xf.reshape(batch, seq, d)
```

```python
import functools
import math

import jax
import jax.numpy as jnp
from jax import lax
from jax.experimental import pallas as pl
from jax.experimental.pallas import tpu as pltpu

F32 = jnp.float32
BF16 = jnp.bfloat16
I32 = jnp.int32

D_MODEL = 2048
ATTN_WIDTH = 1024
POOL_WIDTH = 1024
HEAD_DIM = 64
N_HEADS = 8
HEAD_WIDTH = 2 * HEAD_DIM
POOL_WINDOWS = (2, 4, 8, 16)
POOL_GROUP_DIM = 256
ROPE_THETA = 10000.0
NORM_EPS = 1e-6
N_EXPERTS = 8
TOP_K = 2
NEG_BIG = -1e30
LOG2_E = math.log2(math.e)

LANES = 128
VMEM_BUDGET = 56 * 1024 * 1024

PROJ_BM = 2048
PROJ_BN = 512
IN_PROJ_BM = 2048
IN_PROJ_BN = 512
QK_SLAB = 512
X_CHUNK = 256
NORM_CHUNK = 128
ROPE_BM = 256
ATTN_BQ = 512
ONES_ROWS = 16
ATTN_STREAMS = 2
POOL_BM = 256
FFN_SUB = 256
FFN_MM = 512
FFN_TRIP = 4
FFN_BF = 256
DENSE_TILE = 2048
MOE_TILE = 2560
GATHER_BM = 256
DMA_UNROLL = 8


def _cparams(sem, vmem=None):
    return pltpu.CompilerParams(dimension_semantics=sem, vmem_limit_bytes=vmem)


def _split_bf16(x):
    hi = x.astype(BF16)
    lo = (x - hi.astype(F32)).astype(BF16)
    return hi, lo


def _rms_rows(x_ref, g_ref, h_ref, rows):
    def body(c, carry):
        r = pl.multiple_of(c * NORM_CHUNK, NORM_CHUNK)
        x = x_ref[pl.ds(r, NORM_CHUNK), :]
        ms = jnp.mean(x * x, axis=-1, keepdims=True)
        h_ref[pl.ds(r, NORM_CHUNK), :] = (x * lax.rsqrt(ms + NORM_EPS) * g_ref[...]).astype(h_ref.dtype)
        return carry
    lax.fori_loop(0, rows // NORM_CHUNK, body, 0)


def _in_proj_kernel(x_hbm, g_ref, w_ref, cos_ref, sin_ref, gain_ref, qk_ref, v_ref, u_ref, h_ref, xbuf, sem,
                    *, n_qk, n_v):
    i = pl.program_id(0)
    j = pl.program_id(1)
    bm = h_ref.shape[0]
    chunk = xbuf.shape[1]
    n_chunks = bm // chunk

    @pl.when(j == 0)
    def _():
        def chunk_copy(c, slot):
            rows = pl.ds(pl.multiple_of(i * bm + c * chunk, chunk), chunk)
            return pltpu.make_async_copy(x_hbm.at[rows], xbuf.at[slot], sem.at[slot])

        chunk_copy(0, 0).start()
        for c in range(n_chunks):
            slot = c % 2
            if c + 1 < n_chunks:
                chunk_copy(c + 1, 1 - slot).start()
            chunk_copy(c, slot).wait()
            _rms_rows(xbuf.at[slot], g_ref, h_ref.at[pl.ds(c * chunk, chunk)], chunk)

    def proj():
        return jnp.dot(h_ref[...], w_ref[...].astype(BF16), preferred_element_type=F32)

    def qk_epilogue(y, rows):
        lane = lax.broadcasted_iota(I32, (1, LANES), 1)
        first_half = (lane % HEAD_DIM) < (HEAD_DIM // 2)
        wide = 2 * LANES
        gr = lax.broadcasted_iota(I32, (wide, wide), 0) // HEAD_DIM
        gc = lax.broadcasted_iota(I32, (wide, wide), 1) // HEAD_DIM
        group_ones = jnp.where(gr == gc, 1.0, 0.0).astype(BF16)
        scale = jnp.where(j < n_qk // 2, HEAD_DIM ** -0.5 * LOG2_E, 1.0)
        cos = cos_ref[rows, :]
        sin_signed = sin_ref[rows, :]
        for wb in range(y.shape[1] // wide):
            x2 = y[:, wb * wide:(wb + 1) * wide]
            ssum2 = jnp.dot((x2 * x2).astype(BF16), group_ones, preferred_element_type=F32)
            for hb in range(2):
                cols = slice(wb * wide + hb * LANES, wb * wide + (hb + 1) * LANES)
                x = x2[:, hb * LANES:(hb + 1) * LANES]
                ssum = ssum2[:, hb * LANES:(hb + 1) * LANES]
                z = x * lax.rsqrt(ssum * (1.0 / HEAD_DIM) + NORM_EPS) * gain_ref[:, cols]
                swapped = jnp.where(first_half, pltpu.roll(z, LANES - HEAD_DIM // 2, 1),
                                    pltpu.roll(z, HEAD_DIM // 2, 1))
                qk_ref[rows, cols] = ((z * cos + swapped * sin_signed) * scale).astype(BF16)

    @pl.when(j < n_qk)
    def _():
        w = w_ref[...].astype(BF16)
        for r0 in range(0, bm, QK_SLAB):
            rows = slice(r0, r0 + QK_SLAB)
            qk_epilogue(jnp.dot(h_ref[rows, :], w, preferred_element_type=F32), rows)

    @pl.when((j >= n_qk) & (j < n_qk + n_v))
    def _():
        v_ref[...] = proj().astype(BF16)

    @pl.when(j >= n_qk + n_v)
    def _():
        u_ref[...] = proj()


def _in_proj(x, g, w_all, layer, cos, sin_signed, qk_gain):
    m, d = x.shape
    bn = IN_PROJ_BN
    bm = IN_PROJ_BM
    n_qk = 2 * ATTN_WIDTH // bn
    n_v = ATTN_WIDTH // bn
    n_u = POOL_WIDTH // bn
    assert w_all.shape[2] == (n_qk + n_v + n_u) * bn and m % bm == 0
    return pl.pallas_call(
        functools.partial(_in_proj_kernel, n_qk=n_qk, n_v=n_v),
        grid=(m // bm, n_qk + n_v + n_u),
        in_specs=[
            pl.BlockSpec(memory_space=pl.ANY),
            pl.BlockSpec((1, d), lambda i, j: (0, 0)),
            pl.BlockSpec((None, d, bn), lambda i, j: (layer, 0, j)),
            pl.BlockSpec((bm, LANES), lambda i, j: (i, 0)),
            pl.BlockSpec((bm, LANES), lambda i, j: (i, 0)),
            pl.BlockSpec((1, bn), lambda i, j: (0, jnp.minimum(j, n_qk - 1))),
        ],
        out_specs=[
            pl.BlockSpec((bm, bn), lambda i, j: (i, jnp.minimum(j, n_qk - 1))),
            pl.BlockSpec((bm, bn), lambda i, j: (i, jnp.clip(j - n_qk, 0, n_v - 1))),
            pl.BlockSpec((bm, bn), lambda i, j: (i, jnp.clip(j - n_qk - n_v, 0, n_u - 1))),
        ],
        out_shape=[
            jax.ShapeDtypeStruct((m, 2 * ATTN_WIDTH), BF16),
            jax.ShapeDtypeStruct((m, ATTN_WIDTH), BF16),
            jax.ShapeDtypeStruct((m, POOL_WIDTH), F32),
        ],
        scratch_shapes=[
            pltpu.VMEM((bm, d), BF16),
            pltpu.VMEM((2, X_CHUNK, d), F32),
            pltpu.SemaphoreType.DMA((2,)),
        ],
        compiler_params=_cparams(("arbitrary", "arbitrary"), VMEM_BUDGET),
        name="in_proj",
    )(x, g, w_all, cos, sin_signed, qk_gain)


def _rope_tables_kernel(pos_ref, invf_ref, cos_ref, sin_ref):
    ang = pos_ref[...].astype(F32) * invf_ref[...]
    lane = lax.broadcasted_iota(I32, ang.shape, 1)
    first_half = (lane % HEAD_DIM) < (HEAD_DIM // 2)
    sin = jnp.sin(ang)
    cos_ref[...] = jnp.cos(ang)
    sin_ref[...] = jnp.where(first_half, -sin, sin)


def _rope_tables(pos, invf):
    m = pos.shape[0]
    table = jax.ShapeDtypeStruct((m, LANES), F32)
    return pl.pallas_call(
        _rope_tables_kernel,
        grid=(m // ROPE_BM,),
        in_specs=[
            pl.BlockSpec((ROPE_BM, 1), lambda i: (i, 0)),
            pl.BlockSpec((1, LANES), lambda i: (0, 0)),
        ],
        out_specs=[pl.BlockSpec((ROPE_BM, LANES), lambda i: (i, 0))] * 2,
        out_shape=[table, table],
        compiler_params=_cparams(("arbitrary",)),
        name="rope_tables",
    )(pos, invf)


def _block_streams(nq):
    chains = [[(qi, kb) for kb in range(qi + 1)] for qi in range(nq - 1, -1, -1)]
    streams = tuple([] for _ in range(ATTN_STREAMS))
    for chain in chains:
        min(streams, key=len).extend(chain)
    order = []
    for i in range(max(len(s) for s in streams)):
        order.extend(s[i] for s in streams if i < len(s))
    return order


def _attn_kernel(linit_ref, q_ref, k_ref, v_ref, lq1_ref, lk1_ref, lq2_ref, lk2_ref, sg_ref, o_ref,
                 q2_ref, vt_ref, m_ref, acc_ref):
    nq, _, bq = vt_ref.shape
    lane = lax.broadcasted_iota(I32, (bq, HEAD_WIDTH), 1)
    for c in range(nq):
        rows = slice(c * bq, (c + 1) * bq)
        vt_ref[c, 0:HEAD_WIDTH, :] = v_ref[rows, :].astype(F32).T.astype(BF16)
        vt_ref[c, HEAD_WIDTH:, :] = jnp.ones((ONES_ROWS, bq), BF16)
        q = q_ref[rows, :]
        zero = jnp.zeros_like(q)
        q2_ref[c, 0:bq, :] = jnp.where(lane < HEAD_DIM, q, zero)
        q2_ref[c, bq:2 * bq, :] = jnp.where(lane >= HEAD_DIM, q, zero)
    m_ref[...] = jnp.full(m_ref.shape, NEG_BIG, F32)
    acc_ref[...] = jnp.zeros(acc_ref.shape, F32)

    for qi, kb in _block_streams(nq):
        k = k_ref[kb * bq:(kb + 1) * bq, :]
        s = lax.dot_general(k, q2_ref[qi], (((1,), (1,)), ((), ())), preferred_element_type=F32)
        if kb == qi:
            key = lax.broadcasted_iota(I32, s.shape, 0)
            qry = lax.broadcasted_iota(I32, s.shape, 1) & (bq - 1)
            s = jnp.where(key <= qry, s, NEG_BIG)
        m_old = m_ref[qi]
        m_new = jnp.maximum(m_old, jnp.max(s, axis=0, keepdims=True))
        alpha = jnp.exp2(m_old - m_new)
        p = jnp.exp2(s - m_new)
        acc_ref[qi] = alpha * acc_ref[qi] + jnp.dot(vt_ref[kb], p.astype(BF16), preferred_element_type=F32)
        m_ref[qi] = m_new

    lambda_init = linit_ref[0]
    lam = (jnp.exp(jnp.sum(lq1_ref[...] * lk1_ref[...], axis=-1, keepdims=True))
           - jnp.exp(jnp.sum(lq2_ref[...] * lk2_ref[...], axis=-1, keepdims=True))
           + lambda_init)
    for qi in range(nq):
        acc = acc_ref[qi, 0:HEAD_WIDTH, :]
        l = acc_ref[qi, HEAD_WIDTH:HEAD_WIDTH + 1, :]
        o = acc[:, :bq] / l[:, :bq] - lam * (acc[:, bq:] / l[:, bq:])
        ms = jnp.mean(o * o, axis=0, keepdims=True)
        o = o * lax.rsqrt(ms + NORM_EPS) * sg_ref[...] * (1.0 - lambda_init)
        o_ref[qi * bq:(qi + 1) * bq, :] = o.T.astype(BF16)


def _attention(linit, qk, v, lq1, lk1, lq2, lk2, sg_col, batch, seq):
    m = qk.shape[0]
    nq = seq // ATTN_BQ
    vec = pl.BlockSpec((1, HEAD_DIM), lambda b, h: (0, 0))
    return pl.pallas_call(
        _attn_kernel,
        grid=(batch, N_HEADS),
        in_specs=[
            pl.BlockSpec(memory_space=pltpu.SMEM),
            pl.BlockSpec((seq, HEAD_WIDTH), lambda b, h: (b, h)),
            pl.BlockSpec((seq, HEAD_WIDTH), lambda b, h: (b, N_HEADS + h)),
            pl.BlockSpec((seq, HEAD_WIDTH), lambda b, h: (b, h)),
            vec, vec, vec, vec,
            pl.BlockSpec((HEAD_WIDTH, 1), lambda b, h: (0, 0)),
        ],
        out_specs=pl.BlockSpec((seq, HEAD_WIDTH), lambda b, h: (b, h)),
        out_shape=jax.ShapeDtypeStruct((m, ATTN_WIDTH), BF16),
        scratch_shapes=[
            pltpu.VMEM((nq, 2 * ATTN_BQ, HEAD_WIDTH), BF16),
            pltpu.VMEM((nq, HEAD_WIDTH + ONES_ROWS, ATTN_BQ), BF16),
            pltpu.VMEM((nq, 1, 2 * ATTN_BQ), F32),
            pltpu.VMEM((nq, HEAD_WIDTH + ONES_ROWS, 2 * ATTN_BQ), F32),
        ],
        compiler_params=_cparams(("arbitrary", "arbitrary")),
        name="diff_attn",
    )(linit, qk, qk, v, lq1, lk1, lq2, lk2, sg_col)


def _pool_kernel(uc_ref, up_ref, w_ref, b_ref, sc_ref, o_ref, *, chunks_per_seq):
    c = pl.program_id(0) % chunks_per_seq
    bm = uc_ref.shape[0]
    row = lax.broadcasted_iota(I32, (bm, bm), 0)
    col = lax.broadcasted_iota(I32, (bm, bm), 1)
    t = c * bm + lax.broadcasted_iota(I32, (bm, 1), 0)
    has_prev = c > 0
    halo = max(POOL_WINDOWS)
    hrow = lax.broadcasted_iota(I32, (halo, halo), 0)
    hcol = lax.broadcasted_iota(I32, (halo, halo), 1)
    for g, win in enumerate(POOL_WINDOWS):
        cols = slice(g * POOL_GROUP_DIM, (g + 1) * POOL_GROUP_DIM)
        cur = jnp.where((row >= col) & (row - col < win), 1.0, 0.0).astype(BF16)
        prv = jnp.where(hcol - hrow > halo - win, 1.0, 0.0).astype(BF16)
        u = uc_ref[:, cols]
        u_hi, u_lo = _split_bf16(u)
        p_hi, p_lo = _split_bf16(up_ref[bm - halo:, cols])
        wsum = (jnp.dot(cur, u_hi, preferred_element_type=F32)
                + jnp.dot(cur, u_lo, preferred_element_type=F32))
        wprev = (jnp.dot(prv, p_hi, preferred_element_type=F32)
                 + jnp.dot(prv, p_lo, preferred_element_type=F32))
        wsum = jnp.concatenate([wsum[:halo] + jnp.where(has_prev, wprev, 0.0), wsum[halo:]], axis=0)
        cnt = jnp.minimum(t + 1, win).astype(F32)
        d = wsum / cnt - u
        y = jnp.dot(d.astype(BF16), w_ref[g].astype(BF16), preferred_element_type=F32) + b_ref[:, cols]
        o_ref[:, cols] = (y * sc_ref[:, cols]).astype(BF16)


def _pool(u, w_all, layer, b, sc, seq):
    m, width = u.shape
    cps = seq // POOL_BM
    return pl.pallas_call(
        functools.partial(_pool_kernel, chunks_per_seq=cps),
        grid=(m // POOL_BM,),
        in_specs=[
            pl.BlockSpec((POOL_BM, width), lambda i: (i, 0)),
            pl.BlockSpec((POOL_BM, width), lambda i: (jnp.maximum(i - 1, 0), 0)),
            pl.BlockSpec((None,) + w_all.shape[1:], lambda i: (layer, 0, 0, 0)),
            pl.BlockSpec((1, width), lambda i: (0, 0)),
            pl.BlockSpec((1, width), lambda i: (0, 0)),
        ],
        out_specs=pl.BlockSpec((POOL_BM, width), lambda i: (i, 0)),
        out_shape=jax.ShapeDtypeStruct((m, width), BF16),
        compiler_params=_cparams(("arbitrary",)),
        name="pool_mixer",
    )(u, u, w_all, b, sc)


def _out_proj_kernel(a_ref, p_ref, wa_ref, wp_ref, x_ref, o_ref):
    y = jnp.dot(a_ref[...], wa_ref[...].astype(BF16), preferred_element_type=F32)
    y = y + jnp.dot(p_ref[...], wp_ref[...].astype(BF16), preferred_element_type=F32)
    o_ref[...] = x_ref[...] + y


def _out_proj(a, p, w_all, layer, x):
    m, d = x.shape
    ka = a.shape[1]
    kp = p.shape[1]
    assert ka == kp
    return pl.pallas_call(
        _out_proj_kernel,
        grid=(m // PROJ_BM, d // PROJ_BN),
        in_specs=[
            pl.BlockSpec((PROJ_BM, ka), lambda i, j: (i, 0)),
            pl.BlockSpec((PROJ_BM, kp), lambda i, j: (i, 0)),
            pl.BlockSpec((None, ka, PROJ_BN), lambda i, j: (layer, 0, j)),
            pl.BlockSpec((None, kp, PROJ_BN), lambda i, j: (layer, 1, j)),
            pl.BlockSpec((PROJ_BM, PROJ_BN), lambda i, j: (i, j)),
        ],
        out_specs=pl.BlockSpec((PROJ_BM, PROJ_BN), lambda i, j: (i, j)),
        out_shape=jax.ShapeDtypeStruct((m, d), F32),
        compiler_params=_cparams(("arbitrary", "arbitrary"), VMEM_BUDGET),
        name="out_proj",
    )(a, p, w_all, w_all, x)


def _route_kernel(x_ref, g_ref, rw_ref, h_ref, idx_ref, gate_ref):
    x = x_ref[...]
    ms = jnp.mean(x * x, axis=-1, keepdims=True)
    h = x * lax.rsqrt(ms + NORM_EPS) * g_ref[...]
    h_hi, h_lo = _split_bf16(h)
    half = h.shape[1] // 2
    bits = lax.bitcast_convert_type(h_hi.astype(F32), jnp.uint32)
    h_ref[...] = (bits[:, half:] & jnp.uint32(0xFFFF0000)) | (bits[:, :half] >> 16)
    w_hi, w_lo = _split_bf16(rw_ref[...])
    logits = (jnp.dot(h_hi, w_hi, preferred_element_type=F32)
              + jnp.dot(h_hi, w_lo, preferred_element_type=F32)
              + jnp.dot(h_lo, w_hi, preferred_element_type=F32))
    lane = lax.broadcasted_iota(I32, logits.shape, 1)
    logits = jnp.where(lane < N_EXPERTS, logits, NEG_BIG)
    v1 = jnp.max(logits, axis=-1, keepdims=True)
    i1 = jnp.min(jnp.where(logits == v1, lane, LANES), axis=-1, keepdims=True)
    rest = jnp.where(lane == i1, NEG_BIG, logits)
    v2 = jnp.max(rest, axis=-1, keepdims=True)
    i2 = jnp.min(jnp.where(rest == v2, lane, LANES), axis=-1, keepdims=True)
    e = jnp.exp(v2 - v1)
    g1 = 1.0 / (1.0 + e)
    g2 = e / (1.0 + e)
    idx_ref[...] = jnp.where(lane == 0, i1, jnp.where(lane == 1, i2, 0))
    gate_ref[...] = jnp.where(lane == 0, g1, jnp.where(lane == 1, g2, 0.0))


def _route(x, g, rw_padded):
    m, d = x.shape
    bm = 256
    return pl.pallas_call(
        _route_kernel,
        grid=(m // bm,),
        in_specs=[
            pl.BlockSpec((bm, d), lambda i: (i, 0)),
            pl.BlockSpec((1, d), lambda i: (0, 0)),
            pl.BlockSpec((d, LANES), lambda i: (0, 0)),
        ],
        out_specs=[
            pl.BlockSpec((bm, d // 2), lambda i: (i, 0)),
            pl.BlockSpec((bm, LANES), lambda i: (i, 0)),
            pl.BlockSpec((bm, LANES), lambda i: (i, 0)),
        ],
        out_shape=[
            jax.ShapeDtypeStruct((m, d // 2), jnp.uint32),
            jax.ShapeDtypeStruct((m, LANES), I32),
            jax.ShapeDtypeStruct((m, LANES), F32),
        ],
        compiler_params=_cparams(("arbitrary",)),
        name="ffn_norm_route",
    )(x, g, rw_padded)


def _swiglu_kernel(eid_ref, nsub_ref, blk_ref, tot_ref, x_ref, w1_ref, w3_ref, w2_ref, *rest, has_res):
    if has_res:
        gain_ref, o_ref, w1s, w3s, w2s, h_ref, sem = rest
    else:
        o_ref, w1s, w3s, w2s = rest
        h_ref = x_ref
    t = pl.program_id(0)
    j = pl.program_id(1)
    nsub = nsub_ref[t]
    tile_rows = o_ref.shape[0]

    @pl.when(j == 0)
    def _():
        if has_res:
            seed = pltpu.make_async_copy(
                x_ref.at[pl.ds(pl.multiple_of(t * tile_rows, tile_rows), tile_rows)], o_ref, sem)
            seed.start()
            seed.wait()
            _rms_rows(o_ref, gain_ref, h_ref, tile_rows)
        else:
            def init(i, carry):
                rows = pl.ds(pl.multiple_of(i * FFN_SUB, FFN_SUB), FFN_SUB)
                o_ref[rows, :] = jnp.zeros((FFN_SUB, o_ref.shape[1]), F32)
                return carry
            lax.fori_loop(0, tile_rows // FFN_SUB, init, 0)

    def row_block(start, n_rows):
        rows = pl.ds(pl.multiple_of(start, FFN_SUB), n_rows)
        xs = h_ref[rows, :]
        a = jnp.dot(xs, w1s[...], preferred_element_type=F32)
        b = jnp.dot(xs, w3s[...], preferred_element_type=F32)
        hidden = (a * jax.nn.sigmoid(a) * b).astype(BF16)
        o_ref[rows, :] += jnp.dot(hidden, w2s[...], preferred_element_type=F32)

    def cast_weights():
        w1s[...] = w1_ref[0].astype(BF16)
        w3s[...] = w3_ref[0].astype(BF16)
        w2s[...] = w2_ref[0].astype(BF16)

    if has_res:
        cast_weights()
        for i in range(o_ref.shape[0] // FFN_MM):
            row_block(i * FFN_MM, FFN_MM)
    else:
        @pl.when(nsub > 0)
        def _():
            n_mm = nsub // (FFN_MM // FFN_SUB)
            n_trips = n_mm // FFN_TRIP

            def trip(i, carry):
                for k in range(FFN_TRIP):
                    row_block((i * FFN_TRIP + k) * FFN_MM, FFN_MM)
                return carry

            @pl.when(n_trips > 0)
            def _():
                cast_weights()
                trip(0, 0)

            @pl.when(n_trips == 0)
            def _():
                cast_weights()

            lax.fori_loop(1, n_trips, trip, 0)

            def single(i, carry):
                row_block(i * FFN_MM, FFN_MM)
                return carry
            lax.fori_loop(n_trips * FFN_TRIP, n_mm, single, 0)

            @pl.when(nsub % (FFN_MM // FFN_SUB) == 1)
            def _():
                row_block((nsub - 1) * FFN_SUB, FFN_SUB)


def _swiglu(meta, x, w1, w3, w2, gain, tile_rows):
    eid, nsub, blk, tot = meta
    n_tiles = eid.shape[0]
    d = x.shape[1]
    f = w1.shape[2]
    n_j = f // FFN_BF
    has_res = gain is not None

    def row_map(t, j, eid, nsub, blk, tot):
        return (blk[t], 0)

    def out_map(t, j, eid, nsub, blk, tot):
        return (t, 0)

    def hidden_block(t, j, tot):
        return jnp.where(t < tot[0], j, n_j - 1)

    def w13_map(t, j, eid, nsub, blk, tot):
        return (eid[t], 0, hidden_block(t, j, tot))

    def w2_map(t, j, eid, nsub, blk, tot):
        return (eid[t], hidden_block(t, j, tot), 0)

    once = pl.Buffered(1)
    in_specs = [
        pl.BlockSpec(memory_space=pl.ANY) if has_res else pl.BlockSpec((tile_rows, d), row_map, pipeline_mode=once),
        pl.BlockSpec((1, d, FFN_BF), w13_map),
        pl.BlockSpec((1, d, FFN_BF), w13_map),
        pl.BlockSpec((1, FFN_BF, d), w2_map),
    ]
    args = [x, w1, w3, w2]
    scratch = [
        pltpu.VMEM((d, FFN_BF), BF16),
        pltpu.VMEM((d, FFN_BF), BF16),
        pltpu.VMEM((FFN_BF, d), BF16),
    ]
    if has_res:
        in_specs.append(pl.BlockSpec((1, d), lambda t, j, *_: (0, 0)))
        args.append(gain)
        scratch.append(pltpu.VMEM((tile_rows, d), BF16))
        scratch.append(pltpu.SemaphoreType.DMA(()))
    return pl.pallas_call(
        functools.partial(_swiglu_kernel, has_res=has_res),
        grid_spec=pltpu.PrefetchScalarGridSpec(
            num_scalar_prefetch=4,
            grid=(n_tiles, n_j),
            in_specs=in_specs,
            out_specs=pl.BlockSpec((tile_rows, d), out_map, pipeline_mode=once),
            scratch_shapes=scratch,
        ),
        out_shape=jax.ShapeDtypeStruct((n_tiles * tile_rows, d), F32),
        compiler_params=_cparams(("arbitrary", "arbitrary"), VMEM_BUDGET),
        name="swiglu_res" if has_res else "swiglu_moe",
    )(eid, nsub, blk, tot, *args)


def _gather_kernel(nvalid_ref, tok_ref, h_ref, o_ref, buf):
    i = pl.program_id(0)
    n = o_ref.shape[0]
    valid = nvalid_ref[i] > 0

    @pl.when(valid)
    def _():
        def pick(g, c):
            for k in range(DMA_UNROLL):
                r = g * DMA_UNROLL + k
                buf[pl.ds(r, 1), :] = h_ref[pl.ds(tok_ref[0, 0, r], 1), :]
            return c
        lax.fori_loop(0, n // DMA_UNROLL, pick, 0)
        words = buf[...]
        half = words.shape[1]
        o_ref[:, :half] = lax.bitcast_convert_type(words << 16, F32).astype(BF16)
        o_ref[:, half:] = lax.bitcast_convert_type(words & jnp.uint32(0xFFFF0000), F32).astype(BF16)

    @pl.when(jnp.logical_not(valid))
    def _():
        o_ref[...] = jnp.zeros(o_ref.shape, BF16)


def _gather_rows(nvalid_blocks, tok_of_row, h):
    n_rows = tok_of_row.shape[0]
    m, d = h.shape
    nb = n_rows // GATHER_BM
    tok3 = tok_of_row.reshape(nb, 1, GATHER_BM)
    return pl.pallas_call(
        _gather_kernel,
        grid_spec=pltpu.PrefetchScalarGridSpec(
            num_scalar_prefetch=1,
            grid=(nb,),
            in_specs=[
                pl.BlockSpec((1, 1, GATHER_BM), lambda i, nv: (i, 0, 0), memory_space=pltpu.SMEM),
                pl.BlockSpec((m, d), lambda i, nv: (0, 0), pipeline_mode=pl.Buffered(1)),
            ],
            out_specs=pl.BlockSpec((GATHER_BM, 2 * d), lambda i, nv: (i, 0)),
            scratch_shapes=[pltpu.VMEM((GATHER_BM, d), jnp.uint32)],
        ),
        out_shape=jax.ShapeDtypeStruct((n_rows, 2 * d), BF16),
        compiler_params=_cparams(("arbitrary",), VMEM_BUDGET),
        name="moe_gather",
    )(nvalid_blocks, tok3, h)


def _combine_kernel(pos_ref, pos_next_ref, x_ref, gate_ref, y_hbm, o_ref, buf, sem):
    i = pl.program_id(0)
    nb = pl.num_programs(0)
    n = o_ref.shape[0]
    slot = i % 2

    def row_copy(idx_ref, r, k, s):
        return pltpu.make_async_copy(y_hbm.at[pl.ds(idx_ref[0, k, r], 1)], buf.at[s, k, pl.ds(r, 1)], sem.at[s])

    def start_all(idx_ref, s):
        def start(g, c):
            for k in range(DMA_UNROLL):
                r = g * DMA_UNROLL + k
                row_copy(idx_ref, r, 0, s).start(priority=0)
                row_copy(idx_ref, r, 1, s).start(priority=1)
            return c
        lax.fori_loop(0, n // DMA_UNROLL, start, 0)

    @pl.when(i == 0)
    def _():
        start_all(pos_ref, 0)

    @pl.when(i + 1 < nb)
    def _():
        start_all(pos_next_ref, 1 - slot)

    one_row = pltpu.make_async_copy(y_hbm.at[pl.ds(0, 1)], buf.at[slot, 0, pl.ds(0, 1)], sem.at[slot])

    def wait(r, c):
        one_row.wait()
        one_row.wait()
        return c
    lax.fori_loop(0, n, wait, 0, unroll=DMA_UNROLL)
    g = gate_ref[...]
    o_ref[...] = x_ref[...] + g[:, 0:1] * buf[slot, 0] + g[:, 1:2] * buf[slot, 1]


def _combine(pos, x, gate, y):
    m, d = x.shape
    nb = m // GATHER_BM
    pos3 = pos.reshape(nb, GATHER_BM, TOP_K).transpose(0, 2, 1)
    idx_block = (1, TOP_K, GATHER_BM)
    return pl.pallas_call(
        _combine_kernel,
        grid=(nb,),
        in_specs=[
            pl.BlockSpec(idx_block, lambda i: (i, 0, 0), memory_space=pltpu.SMEM),
            pl.BlockSpec(idx_block, lambda i: (jnp.minimum(i + 1, nb - 1), 0, 0), memory_space=pltpu.SMEM),
            pl.BlockSpec((GATHER_BM, d), lambda i: (i, 0)),
            pl.BlockSpec((GATHER_BM, LANES), lambda i: (i, 0)),
            pl.BlockSpec(memory_space=pl.ANY),
        ],
        out_specs=pl.BlockSpec((GATHER_BM, d), lambda i: (i, 0)),
        out_shape=jax.ShapeDtypeStruct((m, d), F32),
        scratch_shapes=[pltpu.VMEM((2, TOP_K, GATHER_BM, d), F32), pltpu.SemaphoreType.DMA((2,))],
        compiler_params=_cparams(("arbitrary",)),
        name="moe_combine",
    )(pos3, pos3, x, gate, y)


def _dispatch_plan(idx, n_tokens):
    n_assign = n_tokens * TOP_K
    n_tiles = n_assign // MOE_TILE + N_EXPERTS
    e_flat = idx.reshape(n_assign)
    onehot = (e_flat[:, None] == jnp.arange(N_EXPERTS, dtype=I32)[None, :]).astype(I32)
    counts = jnp.sum(onehot, axis=0)
    rank = jnp.take_along_axis(jnp.cumsum(onehot, axis=0) - onehot, e_flat[:, None], axis=1)[:, 0]
    tiles_e = (counts + MOE_TILE - 1) // MOE_TILE
    tile_end = jnp.cumsum(tiles_e)
    tile_start = tile_end - tiles_e
    total = tile_end[-1]
    pos = tile_start[e_flat] * MOE_TILE + rank
    t = jnp.arange(n_tiles, dtype=I32)
    t_eff = jnp.minimum(t, total - 1)
    eid = jnp.minimum(jnp.sum((t_eff[:, None] >= tile_end[None, :]).astype(I32), axis=1), N_EXPERTS - 1)
    rows_valid = jnp.clip(counts[eid] - (t_eff - tile_start[eid]) * MOE_TILE, 0, MOE_TILE)
    rows_valid = jnp.where(t < total, rows_valid, 0)
    nsub = (rows_valid + FFN_SUB - 1) // FFN_SUB
    sub_per_tile = MOE_TILE // GATHER_BM
    sub_id = jnp.arange(n_tiles * sub_per_tile, dtype=I32)
    nvalid_blocks = ((sub_id % sub_per_tile) < nsub[sub_id // sub_per_tile]).astype(I32)
    tok_of_row = jnp.zeros((n_tiles * MOE_TILE,), I32).at[pos].set(jnp.arange(n_assign, dtype=I32) // TOP_K)
    meta = (eid, nsub.astype(I32), t_eff, jnp.reshape(total, (1,)).astype(I32))
    return meta, nvalid_blocks, tok_of_row, pos.reshape(n_tokens, TOP_K)


def _moe(x, g, rw, w1_all, w3_all, w2_all, layer_idx):
    m, d = x.shape
    w1 = w1_all.reshape((-1,) + w1_all.shape[2:])
    w3 = w3_all.reshape((-1,) + w3_all.shape[2:])
    w2 = w2_all.reshape((-1,) + w2_all.shape[2:])
    rw_padded = jnp.zeros((d, LANES), F32).at[:, :N_EXPERTS].set(rw)
    h, idx, gate = _route(x, g, rw_padded)
    meta, nvalid_blocks, tok_of_row, pos = _dispatch_plan(idx[:, :TOP_K], m)
    meta = (meta[0] + layer_idx * N_EXPERTS,) + meta[1:]
    xs = _gather_rows(nvalid_blocks, tok_of_row, h)
    y = _swiglu(meta, xs, w1, w3, w2, None, MOE_TILE)
    return _combine(pos, x, gate, y)


def _dense(x, g, w1_all, w3_all, w2_all, layer_idx):
    m, d = x.shape
    n_tiles = m // DENSE_TILE
    meta = (jnp.full((n_tiles,), layer_idx, I32), jnp.full((n_tiles,), DENSE_TILE // FFN_SUB, I32),
            jnp.arange(n_tiles, dtype=I32), jnp.full((1,), n_tiles, I32))
    return _swiglu(meta, x, w1_all, w3_all, w2_all, g, DENSE_TILE)


def kernel(x, positions, attn_norm, w_in, q_norm, k_norm, lambda_q1, lambda_k1, lambda_q2, lambda_k2, subln, pool_w, pool_b, pool_scale, w_out, ffn_norm, dense_w1, dense_w3, dense_w2, router_w, moe_w1, moe_w3, moe_w2):
    batch, seq, d = x.shape
    depth = w_in.shape[0]
    m = batch * seq
    xf = x.reshape(m, d)
    pos = positions.reshape(m, 1)
    inv_freq = 1.0 / (ROPE_THETA ** (jnp.arange(0, HEAD_DIM, 2, dtype=F32) / HEAD_DIM))
    invf = jnp.tile(inv_freq, LANES // (HEAD_DIM // 2))[None, :]
    cos, sin_signed = _rope_tables(pos, invf)
    for l in range(depth):
        gain = jnp.concatenate([jnp.tile(q_norm[l], ATTN_WIDTH // HEAD_DIM),
                                jnp.tile(k_norm[l], ATTN_WIDTH // HEAD_DIM)])[None, :]
        qk, v, u = _in_proj(xf, attn_norm[l][None, :], w_in, l, cos, sin_signed, gain)
        lambda_init = 0.8 - 0.6 * math.exp(-0.3 * l)
        a = _attention(jnp.full((1,), lambda_init, F32), qk, v, lambda_q1[l][None, :], lambda_k1[l][None, :],
                       lambda_q2[l][None, :], lambda_k2[l][None, :], subln[l][:, None], batch, seq)
        p = _pool(u, pool_w, l, pool_b[l].reshape(1, POOL_WIDTH), pool_scale[l][None, :], seq)
        xf = _out_proj(a, p, w_out, l, xf)
        i = l // 2
        if l % 2 == 0:
            xf = _dense(xf, ffn_norm[l][None, :], dense_w1, dense_w3, dense_w2, i)
        else:
            xf = _moe(xf, ffn_norm[l][None, :], router_w[i], moe_w1, moe_w3, moe_w2, i)
    return xf.reshape(batch, seq, d)
```
